```python
import math, functools
import jax, jax.numpy as jnp
from jax import lax
import numpy as np

D_MODEL = 2048
BATCH = 16
SEQ = 2048
DEPTH = 4
DEC_BATCH = 4
DEC_SEQ = 4096
PAST_LEN = 128

HEAD_DIM = 128
ROPE_THETA = 10000.0
GRID_W = 64
Q_BLOCK = 128
EPS = 1e-6
NEG = -1e30
N_HEADS_A = D_MODEL // (2 * HEAD_DIM)
DIL_PAIRS = ((128, 1), (512, 4), (2048, 16))
W_A = N_HEADS_A * HEAD_DIM
N_HEADS_B = D_MODEL // (4 * HEAD_DIM)
W_BQK = N_HEADS_B * 2 * HEAD_DIM
W_BV = N_HEADS_B * 2 * HEAD_DIM
AB_IN = 3 * W_A + 2 * W_BQK + W_BV
AB_OUT = W_A + W_BV
N_HEADS_C = D_MODEL // HEAD_DIM
N_KV_C = N_HEADS_C // 4
C_Q = N_HEADS_C * HEAD_DIM
C_KV = N_KV_C * HEAD_DIM
C_IN = C_Q + 2 * C_KV
C_OUT = C_Q
PEER_HEADS = 8
N_KEYS = 128
N_EXPERTS = N_KEYS * N_KEYS
PEER_TOPK = 16
PEER_DKEY = 256
PEER_CHUNK = 128

kernel_name = "hybrid_dilated_diff_axialgqa_peer_encoder"

f32 = jnp.float32


def rmsnorm(x, g):
    xf = x.astype(f32)
    y = xf * lax.rsqrt(jnp.mean(xf * xf, axis=-1, keepdims=True) + EPS)
    return (y * g.astype(f32)).astype(x.dtype)


def rope_tables(pos, dim):
    inv = ROPE_THETA ** (-jnp.arange(0, dim, 2, dtype=f32) / dim)
    ang = pos[:, None] * inv[None, :]
    ang = jnp.concatenate([ang, ang], axis=-1)
    return jnp.cos(ang), jnp.sin(ang)


def apply_rope(x, cos, sin):
    half = x.shape[-1] // 2
    xf = x.astype(f32)
    rot = jnp.concatenate([-xf[..., half:], xf[..., :half]], axis=-1)
    return (xf * cos + rot * sin).astype(x.dtype)


def apply_axial(x, cr, sr, cc, sc):
    h = x.shape[-1] // 2
    return jnp.concatenate([apply_rope(x[..., :h], cr, sr), apply_rope(x[..., h:], cc, sc)], axis=-1)


def split_heads(x, n, dh):
    B, S, _ = x.shape
    return x.reshape(B, S, n, dh).transpose(0, 2, 1, 3)


def merge_heads(x):
    B, H, S, dh = x.shape
    return x.transpose(0, 2, 1, 3).reshape(B, S, H * dh)


def dilated_branch(q, k, v, window, dilation):
    B, H, S, dh = q.shape
    half = window // (2 * dilation)
    L = S // dilation
    blk = half
    nb = -(-L // blk)
    Lp = nb * blk

    def streams(a):
        return a.reshape(B, H, L, dilation, dh).transpose(0, 1, 3, 2, 4)

    qs, ks, vs = streams(q), streams(k), streams(v)
    qb = jnp.pad(qs, ((0, 0), (0, 0), (0, 0), (0, Lp - L), (0, 0))).reshape(B, H, dilation, nb, blk, dh)

    def neigh(a):
        ap = jnp.pad(a, ((0, 0), (0, 0), (0, 0), (blk, Lp - L + blk), (0, 0))).reshape(B, H, dilation, nb + 2, blk, dh)
        return jnp.concatenate([ap[:, :, :, :-2], ap[:, :, :, 1:-1], ap[:, :, :, 2:]], axis=4)

    kb, vb = neigh(ks), neigh(vs)
    qi = jnp.arange(nb)[:, None] * blk + jnp.arange(blk)[None, :]
    ki = (jnp.arange(nb)[:, None] - 1) * blk + jnp.arange(3 * blk)[None, :]
    rel = ki[:, None, :] - qi[:, :, None]
    mask = (jnp.abs(rel) <= half) & (ki[:, None, :] >= 0) & (ki[:, None, :] < L)
    s = jnp.einsum('bhrnqd,bhrnkd->bhrnqk', qb, kb).astype(f32) * (dh ** -0.5)
    s = jnp.where(mask, s, NEG)
    m = jnp.max(s, axis=-1, keepdims=True)
    p = jnp.exp(s - m)
    den = jnp.sum(p, axis=-1, keepdims=True)
    o = jnp.einsum('bhrnqk,bhrnkd->bhrnqd', (p / den).astype(v.dtype), vb)
    lse = (m + jnp.log(den))[..., 0]
    o = o.reshape(B, H, dilation, Lp, dh)[:, :, :, :L].transpose(0, 1, 3, 2, 4).reshape(B, H, S, dh)
    lse = lse.reshape(B, H, dilation, Lp)[:, :, :, :L].transpose(0, 1, 3, 2).reshape(B, H, S)
    return o, lse


def diff_attention(q, k, v, lam):
    B, H, _, S, dh = q.shape
    nq = S // Q_BLOCK
    qb = q.reshape(B, H, 2, nq, Q_BLOCK, dh).transpose(3, 0, 1, 2, 4, 5)

    def block(qblk):
        s = jnp.einsum('bhcqd,bhckd->bhcqk', qblk, k).astype(f32) * (dh ** -0.5)
        p = jax.nn.softmax(s, axis=-1)
        a = p[:, :, 0] - lam * p[:, :, 1]
        return jnp.einsum('bhqk,bhkd->bhqd', a.astype(v.dtype), v)

    o = lax.map(block, qb)
    return o.transpose(1, 2, 0, 3, 4).reshape(B, H, S, v.shape[-1])


def mixer_ab(h, w_in, w_out, lq1, lk1, lq2, lk2, g_sub, lam_init, cos, sin):
    B, S, _ = h.shape
    proj = h @ w_in
    c = np.cumsum([W_A, W_A, W_A, W_BQK, W_BQK])
    qa, ka, va, qbp, kbp, vbp = jnp.split(proj, [int(i) for i in c], axis=-1)
    qa = apply_rope(split_heads(qa, N_HEADS_A, HEAD_DIM), cos, sin)
    ka = apply_rope(split_heads(ka, N_HEADS_A, HEAD_DIM), cos, sin)
    va = split_heads(va, N_HEADS_A, HEAD_DIM)
    outs, lses = [], []
    for (w, d) in DIL_PAIRS:
        o_i, l_i = dilated_branch(qa, ka, va, w, d)
        outs.append(o_i)
        lses.append(l_i)
    wts = jax.nn.softmax(jnp.stack(lses, 0), axis=0)
    oa = jnp.einsum('nbhs,nbhsd->bhsd', wts, jnp.stack(outs, 0).astype(f32)).astype(h.dtype)
    qb = qbp.reshape(B, S, N_HEADS_B, 2, HEAD_DIM).transpose(0, 2, 3, 1, 4)
    kb = kbp.reshape(B, S, N_HEADS_B, 2, HEAD_DIM).transpose(0, 2, 3, 1, 4)
    qb = apply_rope(qb, cos, sin)
    kb = apply_rope(kb, cos, sin)
    vb = split_heads(vbp, N_HEADS_B, 2 * HEAD_DIM)
    lam = (jnp.exp(jnp.sum(lq1.astype(f32) * lk1.astype(f32)))
           - jnp.exp(jnp.sum(lq2.astype(f32) * lk2.astype(f32))) + lam_init)
    ob = diff_attention(qb, kb, vb, lam)
    ob = (rmsnorm(ob, g_sub).astype(f32) * (1.0 - lam_init)).astype(h.dtype)
    return jnp.concatenate([merge_heads(oa), merge_heads(ob)], axis=-1) @ w_out


def mixer_c(h, w_qkv, w_out, g_q, g_k, cr, sr, cc, sc):
    B, S, _ = h.shape
    proj = h @ w_qkv
    q, k, v = jnp.split(proj, [C_Q, C_Q + C_KV], axis=-1)
    q = apply_axial(rmsnorm(split_heads(q, N_HEADS_C, HEAD_DIM), g_q), cr, sr, cc, sc)
    k = apply_axial(rmsnorm(split_heads(k, N_KV_C, HEAD_DIM), g_k), cr, sr, cc, sc)
    v = split_heads(v, N_KV_C, HEAD_DIM)
    G = N_HEADS_C // N_KV_C
    nq = S // Q_BLOCK
    qb = q.reshape(B, N_KV_C, G, nq, Q_BLOCK, HEAD_DIM).transpose(3, 0, 1, 2, 4, 5)

    def block(qblk):
        s = jnp.einsum('bkgqd,bksd->bkgqs', qblk, k).astype(f32) * (HEAD_DIM ** -0.5)
        p = jax.nn.softmax(s, axis=-1).astype(v.dtype)
        return jnp.einsum('bkgqs,bksd->bkgqd', p, v)

    o = lax.map(block, qb)
    o = o.transpose(1, 2, 3, 0, 4, 5).reshape(B, N_HEADS_C, S, HEAD_DIM)
    return merge_heads(o) @ w_out


def peer(x, w_q, sub_keys, u, v):
    B, S, D = x.shape
    T = B * S
    xt = x.reshape(T, D)
    q = (xt @ w_q).reshape(T, PEER_HEADS, 2, PEER_DKEY // 2)
    s = jnp.einsum('thcd,hcnd->thcn', q, sub_keys).astype(f32)
    s1, i1 = lax.top_k(s[:, :, 0], PEER_TOPK)
    s2, i2 = lax.top_k(s[:, :, 1], PEER_TOPK)
    cand = (s1[..., :, None] + s2[..., None, :]).reshape(T, PEER_HEADS, PEER_TOPK * PEER_TOPK)
    cidx = (i1[..., :, None] * N_KEYS + i2[..., None, :]).reshape(T, PEER_HEADS, PEER_TOPK * PEER_TOPK)
    top, pos = lax.top_k(cand, PEER_TOPK)
    idx = jnp.take_along_axis(cidx, pos, axis=-1)
    g = jax.nn.softmax(top, axis=-1)
    nc = T // PEER_CHUNK
    KK = PEER_HEADS * PEER_TOPK

    def chunk(args):
        xc, ic, gc = args
        hc = jax.nn.gelu(jnp.einsum('ckd,cd->ck', u[ic], xc).astype(f32), approximate=False)
        return jnp.einsum('ck,ckd->cd', (gc * hc).astype(x.dtype), v[ic])

    out = lax.map(chunk, (xt.reshape(nc, PEER_CHUNK, D), idx.reshape(nc, PEER_CHUNK, KK),
                          g.reshape(nc, PEER_CHUNK, KK)))
    return out.reshape(B, S, D)


def trunk(x, g_mix, g_ffn, g_final, w_in_ab, w_out_ab, lam_q1, lam_k1, lam_q2, lam_k2, g_subln,
          w_qkv_c, w_out_c, g_qnorm, g_knorm, w_peer_q, peer_sub_keys, peer_u, peer_v):
    S = x.shape[1]
    cos, sin = rope_tables(jnp.arange(S, dtype=f32), HEAD_DIM)
    rows = S // GRID_W
    row_idx = jnp.repeat(jnp.arange(rows, dtype=f32), GRID_W)
    col_idx = jnp.tile(jnp.arange(GRID_W, dtype=f32), rows)
    cr, sr = rope_tables(row_idx, HEAD_DIM // 2)
    cc, sc = rope_tables(col_idx, HEAD_DIM // 2)
    for l in range(DEPTH):
        h = rmsnorm(x, g_mix[l])
        if l % 2 == 0:
            e = l // 2
            lam_init = 0.8 - 0.6 * math.exp(-0.3 * l)
            x = x + mixer_ab(h, w_in_ab[e], w_out_ab[e], lam_q1[e], lam_k1[e], lam_q2[e], lam_k2[e],
                             g_subln[e], lam_init, cos, sin)
        else:
            o = l // 2
            x = x + mixer_c(h, w_qkv_c[o], w_out_c[o], g_qnorm[o], g_knorm[o], cr, sr, cc, sc)
        x = x + peer(rmsnorm(x, g_ffn[l]), w_peer_q[l], peer_sub_keys[l], peer_u[l], peer_v[l])
    return rmsnorm(x, g_final)


def setup_inputs(seed: int = 0) -> dict:
    key = jax.random.key(seed)
    ks = jax.random.split(key, 24)
    NE = (DEPTH + 1) // 2
    NO = DEPTH // 2

    def n(k, shape, scale):
        return jax.random.normal(k, shape, f32) * scale

    return {
        "x_prompt": n(ks[0], (BATCH, SEQ, D_MODEL), 1.0),
        "x_sample": n(ks[1], (DEC_BATCH, DEC_SEQ, D_MODEL), 1.0),
        "g_mix": 1.0 + n(ks[2], (DEPTH, D_MODEL), 0.02),
        "g_ffn": 1.0 + n(ks[3], (DEPTH, D_MODEL), 0.02),
        "g_final": 1.0 + n(ks[4], (D_MODEL,), 0.02),
        "w_in_ab": n(ks[5], (NE, D_MODEL, AB_IN), D_MODEL ** -0.5),
        "w_out_ab": n(ks[6], (NE, AB_OUT, D_MODEL), AB_OUT ** -0.5),
        "lam_q1": n(ks[7], (NE, HEAD_DIM), 0.1),
        "lam_k1": n(ks[8], (NE, HEAD_DIM), 0.1),
        "lam_q2": n(ks[9], (NE, HEAD_DIM), 0.1),
        "lam_k2": n(ks[10], (NE, HEAD_DIM), 0.1),
        "g_subln": 1.0 + n(ks[11], (NE, 2 * HEAD_DIM), 0.02),
        "w_qkv_c": n(ks[12], (NO, D_MODEL, C_IN), D_MODEL ** -0.5),
        "w_out_c": n(ks[13], (NO, C_OUT, D_MODEL), C_OUT ** -0.5),
        "g_qnorm": 1.0 + n(ks[14], (NO, HEAD_DIM), 0.02),
        "g_knorm": 1.0 + n(ks[15], (NO, HEAD_DIM), 0.02),
        "w_peer_q": n(ks[16], (DEPTH, D_MODEL, PEER_HEADS * PEER_DKEY), D_MODEL ** -0.5),
        "peer_sub_keys": n(ks[17], (DEPTH, PEER_HEADS, 2, N_KEYS, PEER_DKEY // 2), (PEER_DKEY // 2) ** -0.5),
        "peer_u": n(ks[18], (DEPTH, N_EXPERTS, D_MODEL), D_MODEL ** -0.5),
        "peer_v": n(ks[19], (DEPTH, N_EXPERTS, D_MODEL), 0.25),
    }


def reference(x_prompt, x_sample, g_mix, g_ffn, g_final, w_in_ab, w_out_ab, lam_q1, lam_k1, lam_q2,
              lam_k2, g_subln, w_qkv_c, w_out_c, g_qnorm, g_knorm, w_peer_q, peer_sub_keys, peer_u, peer_v):
    y_prompt = trunk(x_prompt, g_mix, g_ffn, g_final, w_in_ab, w_out_ab, lam_q1, lam_k1, lam_q2, lam_k2,
                     g_subln, w_qkv_c, w_out_c, g_qnorm, g_knorm, w_peer_q, peer_sub_keys, peer_u, peer_v)
    y_sample = trunk(x_sample, g_mix, g_ffn, g_final, w_in_ab, w_out_ab, lam_q1, lam_k1, lam_q2, lam_k2,
                     g_subln, w_qkv_c, w_out_c, g_qnorm, g_knorm, w_peer_q, peer_sub_keys, peer_u, peer_v)
    return (y_prompt, y_sample)
```

```python
import functools
import math

import jax
import jax.numpy as jnp
from jax import lax
from jax.experimental import pallas as pl
from jax.experimental.pallas import tpu as pltpu

f32 = jnp.float32
bf16 = jnp.bfloat16

HEAD_DIM = 128
LANES = 128
ROPE_THETA = 10000.0
GRID_W = 64
EPS = 1e-6
NEG = -1e30
DIL_PAIRS = ((128, 1), (512, 4), (2048, 16))
PEER_HEADS = 8
N_KEYS = 128
PEER_TOPK = 16
VMEM_LIMIT = 60 * 1024 * 1024

_NT = (((1,), (1,)), ((), ()))


def _largest_divisor(n, candidates):
    for c in candidates:
        if n % c == 0:
            return c
    return n


def _params(*sem):
    return pltpu.CompilerParams(dimension_semantics=sem, vmem_limit_bytes=VMEM_LIMIT)


def _rms(x, g):
    return x * lax.rsqrt(jnp.mean(x * x, axis=-1, keepdims=True) + EPS) * g


def _norm_proj_kernel(x_ref, g_ref, w_ref, *rest, emit_h):
    if emit_h:
        o_ref, hb_ref, h_scr = rest
    else:
        o_ref, h_scr = rest

    @pl.when(pl.program_id(1) == 0)
    def _():
        hb = _rms(x_ref[...], g_ref[...]).astype(bf16)
        h_scr[...] = hb
        if emit_h:
            hb_ref[...] = hb

    o_ref[...] = jnp.dot(h_scr[...], w_ref[...], preferred_element_type=f32).astype(o_ref.dtype)


def _norm_proj(x, g, w, out_dtype, emit_h=False):
    T, D = x.shape
    N = w.shape[1]
    tm = _largest_divisor(T, (1024, 512, 256, 128))
    tn = _largest_divisor(N, (512, 256, 128))
    out_shape = [jax.ShapeDtypeStruct((T, N), out_dtype)]
    out_specs = [pl.BlockSpec((tm, tn), lambda i, j: (i, j))]
    if emit_h:
        out_shape.append(jax.ShapeDtypeStruct((T, D), bf16))
        out_specs.append(pl.BlockSpec((tm, D), lambda i, j: (i, 0)))
    res = pl.pallas_call(
        functools.partial(_norm_proj_kernel, emit_h=emit_h),
        grid=(T // tm, N // tn),
        in_specs=[pl.BlockSpec((tm, D), lambda i, j: (i, 0)),
                  pl.BlockSpec((1, D), lambda i, j: (0, 0)),
                  pl.BlockSpec((D, tn), lambda i, j: (0, j))],
        out_specs=out_specs,
        out_shape=out_shape,
        scratch_shapes=[pltpu.VMEM((tm, D), bf16)],
        compiler_params=_params("parallel", "arbitrary"),
        name="norm_proj",
    )(x, g.reshape(1, D), w)
    return res if emit_h else res[0]


def _out_proj_kernel(*refs, n_in):
    res_ref = refs[0]
    a_refs = refs[1:1 + n_in]
    w_refs = refs[1 + n_in:1 + 2 * n_in]
    o_ref = refs[1 + 2 * n_in]
    acc = res_ref[...]
    for a_ref, w_ref in zip(a_refs, w_refs):
        acc = acc + jnp.dot(a_ref[...], w_ref[...], preferred_element_type=f32)
    o_ref[...] = acc


def _out_proj(res, acts, w):
    T, N = res.shape
    tm = _largest_divisor(T, (1024, 512, 256, 128))
    tn = _largest_divisor(N, (512, 256, 128))
    n_in = len(acts)
    K = acts[0].shape[1]
    assert all(a.shape[1] == K for a in acts) and w.shape[0] == n_in * K
    in_specs = [pl.BlockSpec((tm, tn), lambda i, j: (i, j))]
    in_specs += [pl.BlockSpec((tm, K), lambda i, j: (i, 0)) for _ in acts]
    in_specs += [pl.BlockSpec((K, tn), functools.partial(lambda i, j, kk: (kk, j), kk=kk))
                 for kk in range(n_in)]
    return pl.pallas_call(
        functools.partial(_out_proj_kernel, n_in=n_in),
        grid=(T // tm, N // tn),
        in_specs=in_specs,
        out_specs=pl.BlockSpec((tm, tn), lambda i, j: (i, j)),
        out_shape=jax.ShapeDtypeStruct((T, N), f32),
        compiler_params=_params("parallel", "parallel"),
        name="out_proj",
    )(res, *acts, *([w] * n_in))


def _rope_tables(pos, dim):
    inv = ROPE_THETA ** (-jnp.arange(0, dim, 2, dtype=f32) / dim)
    ang = pos[:, None] * inv[None, :]
    ang = jnp.concatenate([ang, ang], axis=-1)
    return jnp.cos(ang), jnp.sin(ang)


def _rope_tables_full(S):
    cos, sin = _rope_tables(jnp.arange(S, dtype=f32), HEAD_DIM)
    half = HEAD_DIM // 2
    sign = jnp.where(jnp.arange(HEAD_DIM) < half, -1.0, 1.0).astype(f32)
    return cos, sin * sign[None, :]


def _rope_tables_axial(S):
    rows = S // GRID_W
    row_idx = jnp.repeat(jnp.arange(rows, dtype=f32), GRID_W)
    col_idx = jnp.tile(jnp.arange(GRID_W, dtype=f32), rows)
    cr, sr = _rope_tables(row_idx, HEAD_DIM // 2)
    cc, sc = _rope_tables(col_idx, HEAD_DIM // 2)
    quarter = HEAD_DIM // 4
    sign = jnp.where(jnp.arange(HEAD_DIM // 2) < quarter, -1.0, 1.0).astype(f32)
    cos = jnp.concatenate([cr, cc], axis=-1)
    sin = jnp.concatenate([sr * sign[None, :], sc * sign[None, :]], axis=-1)
    return cos, sin


def _rope(x, cos, sin_signed):
    return x * cos + pltpu.roll(x, HEAD_DIM // 2, axis=1) * sin_signed


def _rope_axial(x, cos, sin_signed):
    quarter = HEAD_DIM // 4
    lane = lax.broadcasted_iota(jnp.int32, x.shape, 1)
    lower = (lane % (2 * quarter)) < quarter
    rot = jnp.where(lower, pltpu.roll(x, HEAD_DIM - quarter, axis=1), pltpu.roll(x, quarter, axis=1))
    return x * cos + rot * sin_signed


def _softmax_pv(s, v):
    m = jnp.max(s, axis=-1, keepdims=True)
    p = jnp.exp(s - m)
    l = jnp.sum(p, axis=-1, keepdims=True)
    return jnp.dot(p.astype(bf16), v, preferred_element_type=f32) / l


def _attn_c_kernel(q_ref, k_ref, v_ref, cos_ref, sin_ref, cosq_ref, sinq_ref, gq_ref, gk_ref,
                   o_ref, k_scr, *, groups, scale):
    @pl.when(pl.program_id(2) == 0)
    def _():
        k = _rms(k_ref[0].astype(f32), gk_ref[...])
        k_scr[...] = _rope_axial(k, cos_ref[...], sin_ref[...]).astype(bf16)

    cq = cosq_ref[...]
    sq = sinq_ref[...]
    gq = gq_ref[...]
    v = v_ref[0]
    for g in range(groups):
        cols = slice(g * HEAD_DIM, (g + 1) * HEAD_DIM)
        q = _rms(q_ref[0, :, cols].astype(f32), gq)
        q = _rope_axial(q, cq, sq).astype(bf16)
        s = lax.dot_general(q, k_scr[...], _NT, preferred_element_type=f32) * scale
        o_ref[0, :, cols] = _softmax_pv(s, v).astype(o_ref.dtype)


def _attn_c(proj, g_q, g_k, cos, sin, n_heads, n_kv):
    B, S, _ = proj.shape
    groups = n_heads // n_kv
    tq = _largest_divisor(S, (256, 128))
    qw = groups * HEAD_DIM
    k_off = n_heads
    v_off = n_heads + n_kv
    return pl.pallas_call(
        functools.partial(_attn_c_kernel, groups=groups, scale=HEAD_DIM ** -0.5),
        grid=(B, n_kv, S // tq),
        in_specs=[pl.BlockSpec((1, tq, qw), lambda b, h, i: (b, i, h)),
                  pl.BlockSpec((1, S, HEAD_DIM), lambda b, h, i: (b, 0, k_off + h)),
                  pl.BlockSpec((1, S, HEAD_DIM), lambda b, h, i: (b, 0, v_off + h)),
                  pl.BlockSpec((S, HEAD_DIM), lambda b, h, i: (0, 0)),
                  pl.BlockSpec((S, HEAD_DIM), lambda b, h, i: (0, 0)),
                  pl.BlockSpec((tq, HEAD_DIM), lambda b, h, i: (i, 0)),
                  pl.BlockSpec((tq, HEAD_DIM), lambda b, h, i: (i, 0)),
                  pl.BlockSpec((1, HEAD_DIM), lambda b, h, i: (0, 0)),
                  pl.BlockSpec((1, HEAD_DIM), lambda b, h, i: (0, 0))],
        out_specs=pl.BlockSpec((1, tq, qw), lambda b, h, i: (b, i, h)),
        out_shape=jax.ShapeDtypeStruct((B, S, n_heads * HEAD_DIM), bf16),
        scratch_shapes=[pltpu.VMEM((S, HEAD_DIM), bf16)],
        compiler_params=_params("parallel", "parallel", "arbitrary"),
        name="attn_c",
    )(proj, proj, proj, cos, sin, cos, sin, g_q.reshape(1, HEAD_DIM), g_k.reshape(1, HEAD_DIM))


def _attn_b_kernel(lam_ref, q_ref, k_ref, v_ref, cos_ref, sin_ref, cosq_ref, sinq_ref, gsub_ref,
                   o_ref, k_scr, *, scale, lam_init):
    @pl.when(pl.program_id(2) == 0)
    def _():
        for c in range(2):
            k = k_ref[0, :, c * HEAD_DIM:(c + 1) * HEAD_DIM].astype(f32)
            k_scr[c] = _rope(k, cos_ref[...], sin_ref[...]).astype(bf16)

    lam_p = lam_ref[...]
    lam = (jnp.exp(jnp.sum(lam_p[0:1] * lam_p[1:2], axis=-1, keepdims=True))
           - jnp.exp(jnp.sum(lam_p[2:3] * lam_p[3:4], axis=-1, keepdims=True)) + lam_init)
    cq = cosq_ref[...]
    sq = sinq_ref[...]
    v = v_ref[0]
    outs = []
    for c in range(2):
        q = q_ref[0, :, c * HEAD_DIM:(c + 1) * HEAD_DIM].astype(f32)
        q = _rope(q, cq, sq).astype(bf16)
        s = lax.dot_general(q, k_scr[c], _NT, preferred_element_type=f32) * scale
        outs.append(_softmax_pv(s, v))
    o = outs[0] - lam * outs[1]
    o = _rms(o, gsub_ref[...]) * (1.0 - lam_init)
    o_ref[0] = o.astype(o_ref.dtype)


def _attn_b(proj, lam_params, g_sub, cos, sin, lam_init, q_off, k_off, v_off, n_heads):
    B, S, _ = proj.shape
    tq = _largest_divisor(S, (256, 128))
    hw = 2 * HEAD_DIM
    return pl.pallas_call(
        functools.partial(_attn_b_kernel, scale=HEAD_DIM ** -0.5, lam_init=lam_init),
        grid=(B, n_heads, S // tq),
        in_specs=[pl.BlockSpec((4, HEAD_DIM), lambda b, h, i: (0, 0)),
                  pl.BlockSpec((1, tq, hw), lambda b, h, i: (b, i, q_off + h)),
                  pl.BlockSpec((1, S, hw), lambda b, h, i: (b, 0, k_off + h)),
                  pl.BlockSpec((1, S, hw), lambda b, h, i: (b, 0, v_off + h)),
                  pl.BlockSpec((S, HEAD_DIM), lambda b, h, i: (0, 0)),
                  pl.BlockSpec((S, HEAD_DIM), lambda b, h, i: (0, 0)),
                  pl.BlockSpec((tq, HEAD_DIM), lambda b, h, i: (i, 0)),
                  pl.BlockSpec((tq, HEAD_DIM), lambda b, h, i: (i, 0)),
                  pl.BlockSpec((1, hw), lambda b, h, i: (0, 0))],
        out_specs=pl.BlockSpec((1, tq, hw), lambda b, h, i: (b, i, h)),
        out_shape=jax.ShapeDtypeStruct((B, S, n_heads * hw), bf16),
        scratch_shapes=[pltpu.VMEM((2, S, HEAD_DIM), bf16)],
        compiler_params=_params("parallel", "parallel", "arbitrary"),
        name="attn_b",
    )(lam_params, proj, proj, proj, cos, sin, cos, sin, g_sub.reshape(1, hw))


def _attn_a_kernel(q_ref, k_ref, v_ref, cos_ref, sin_ref, o_ref, qf, kf, vf, *scr, S, scale):
    nb = len(DIL_PAIRS)
    o_scr, l_scr = scr[:nb], scr[nb:]
    cos = cos_ref[...]
    sin = sin_ref[...]
    qf[...] = _rope(q_ref[0].astype(f32), cos, sin)
    kf[...] = _rope(k_ref[0].astype(f32), cos, sin)
    vf[...] = v_ref[0].astype(f32)

    for bi, (window, dil) in enumerate(DIL_PAIRS):
        L = S // dil
        half = window // (2 * dil)
        qc_n = min(128, L)
        kw_n = min(L, qc_n + 2 * half)
        n_chunks = L // qc_n

        def rows(start, n, dil=dil):
            return pl.ds(start, n) if dil == 1 else pl.ds(start, n, stride=dil)

        def body(idx, carry, dil=dil, half=half, qc_n=qc_n, kw_n=kw_n, n_chunks=n_chunks, L=L,
                 bi=bi, rows=rows):
            r = idx // n_chunks
            q0 = (idx % n_chunks) * qc_n
            k0 = jnp.clip(q0 - half, 0, L - kw_n)
            q_rows = rows(r + dil * q0, qc_n)
            k_rows = rows(r + dil * k0, kw_n)
            qc = qf[q_rows, :].astype(bf16)
            kc = kf[k_rows, :].astype(bf16)
            vc = vf[k_rows, :].astype(bf16)
            s = lax.dot_general(qc, kc, _NT, preferred_element_type=f32) * scale
            qpos = q0 + lax.broadcasted_iota(jnp.int32, (qc_n, kw_n), 0)
            kpos = k0 + lax.broadcasted_iota(jnp.int32, (qc_n, kw_n), 1)
            s = jnp.where(jnp.abs(kpos - qpos) <= half, s, NEG)
            m = jnp.max(s, axis=-1, keepdims=True)
            p = jnp.exp(s - m)
            den = jnp.sum(p, axis=-1, keepdims=True)
            o_scr[bi][q_rows, :] = jnp.dot((p / den).astype(bf16), vc, preferred_element_type=f32)
            l_scr[bi][q_rows, :] = jnp.broadcast_to(m + jnp.log(den), (qc_n, HEAD_DIM))
            return carry

        lax.fori_loop(0, dil * n_chunks, body, 0)

    lses = [l_scr[bi][...] for bi in range(nb)]
    m = functools.reduce(jnp.maximum, lses)
    es = [jnp.exp(l - m) for l in lses]
    z = functools.reduce(lambda a, b: a + b, es)
    out = functools.reduce(lambda a, b: a + b, [(e / z) * o_scr[bi][...] for bi, e in enumerate(es)])
    o_ref[0] = out.astype(o_ref.dtype)


def _attn_a(proj, cos, sin, n_heads):
    B, S, _ = proj.shape
    nb = len(DIL_PAIRS)
    blk = lambda off: pl.BlockSpec((1, S, HEAD_DIM), lambda b, h: (b, 0, off + h))
    tab = pl.BlockSpec((S, HEAD_DIM), lambda b, h: (0, 0))
    return pl.pallas_call(
        functools.partial(_attn_a_kernel, S=S, scale=HEAD_DIM ** -0.5),
        grid=(B, n_heads),
        in_specs=[blk(0), blk(n_heads), blk(2 * n_heads), tab, tab],
        out_specs=pl.BlockSpec((1, S, HEAD_DIM), lambda b, h: (b, 0, h)),
        out_shape=jax.ShapeDtypeStruct((B, S, n_heads * HEAD_DIM), bf16),
        scratch_shapes=[pltpu.VMEM((S, HEAD_DIM), f32), pltpu.VMEM((S, HEAD_DIM), f32),
                        pltpu.VMEM((S, HEAD_DIM), f32)]
                       + [pltpu.VMEM((S, HEAD_DIM), f32) for _ in range(2 * nb)],
        compiler_params=_params("parallel", "parallel"),
        name="attn_a",
    )(proj, proj, proj, cos, sin)


def _topk_vals(s, k):
    n_rows = s.shape[0]
    rows = lax.broadcasted_iota(jnp.int32, s.shape, 0)
    out = []
    for _ in range(k):
        m = jnp.max(s, axis=0, keepdims=True)
        first = jnp.min(jnp.where(s == m, rows, n_rows), axis=0, keepdims=True)
        s = jnp.where(rows == first, -jnp.inf, s)
        out.append(m)
    return jnp.concatenate(out, axis=0)


def _peer_retrieve_kernel(q_ref, keys_ref, p_ref, tau_ref):
    K = PEER_TOPK
    scores, tops = [], []
    for c in range(2):
        qc = q_ref[:, c * N_KEYS:(c + 1) * N_KEYS].astype(bf16)
        kc = keys_ref[0, c].astype(bf16)
        s = lax.dot_general(kc, qc, _NT, preferred_element_type=f32)
        scores.append(s)
        tops.append(_topk_vals(s, K))
    v1, v2 = tops
    cand = [v1 + v2[0:1]]
    cand += [v1[0:8] + v2[j:j + 1] for j in range(1, 8)]
    cand += [v1[0:1] + v2[8:16]]
    top = _topk_vals(jnp.concatenate(cand, axis=0), K)
    tau = top[K - 1:K]
    z = jnp.sum(jnp.exp(top - top[0:1]), axis=0, keepdims=True)
    p_ref[0, 0] = scores[0]
    p_ref[0, 1] = jnp.exp(scores[0] - v1[0:1]) / z
    p_ref[0, 2] = scores[1]
    p_ref[0, 3] = jnp.exp(scores[1] - v2[0:1])
    tau_ref[pl.ds(pl.program_id(1), 1), :] = tau


def _peer_retrieve(q, sub_keys):
    T = q.shape[0]
    tm = _largest_divisor(T, (256, 128))
    return pl.pallas_call(
        _peer_retrieve_kernel,
        grid=(T // tm, PEER_HEADS),
        in_specs=[pl.BlockSpec((tm, 2 * N_KEYS), lambda i, h: (i, h)),
                  pl.BlockSpec((1, 2, N_KEYS, HEAD_DIM), lambda i, h: (h, 0, 0, 0))],
        out_specs=[pl.BlockSpec((1, 4, N_KEYS, tm), lambda i, h: (h, 0, 0, i)),
                   pl.BlockSpec((PEER_HEADS, tm), lambda i, h: (0, i))],
        out_shape=[jax.ShapeDtypeStruct((PEER_HEADS, 4, N_KEYS, T), f32),
                   jax.ShapeDtypeStruct((PEER_HEADS, T), f32)],
        compiler_params=_params("parallel", "arbitrary"),
        name="peer_retrieve",
    )(q, sub_keys)


def _gelu(x):
    return 0.5 * x * (1.0 + lax.erf(x * (1.0 / math.sqrt(2.0))))


def _peer_expert_kernel(x_ref, hb_ref, p_ref, tau_ref, u_ref, vt_ref, o_ref, acc_ref, at_ref, *, eb):
    j = pl.program_id(1)

    @pl.when(j == 0)
    def _():
        acc_ref[...] = jnp.zeros_like(acc_ref)

    hb = hb_ref[...]
    for sb in range(eb // N_KEYS):
        a = j * (eb // N_KEYS) + sb
        gate = None
        for h in range(PEER_HEADS):
            s1 = p_ref[h, 0, pl.ds(a, 1), :]
            e1 = p_ref[h, 1, pl.ds(a, 1), :]
            hit = (s1 + p_ref[h, 2]) >= tau_ref[h:h + 1, :]
            w = jnp.where(hit, e1 * p_ref[h, 3], 0.0)
            gate = w if gate is None else gate + w
        rows = slice(sb * N_KEYS, (sb + 1) * N_KEYS)
        ht = lax.dot_general(u_ref[rows, :], hb, _NT, preferred_element_type=f32)
        at_ref[rows, :] = (gate * _gelu(ht)).astype(bf16)
    acc_ref[...] += jnp.dot(vt_ref[...], at_ref[...], preferred_element_type=f32)

    @pl.when(j == pl.num_programs(1) - 1)
    def _():
        o_ref[...] = x_ref[...] + acc_ref[...].T


def _peer_expert(x, hb, p, tau, u, vt):
    T, D = x.shape
    E = u.shape[0]
    tm = _largest_divisor(T, (512, 256, 128))
    eb = 512
    return pl.pallas_call(
        functools.partial(_peer_expert_kernel, eb=eb),
        grid=(T // tm, E // eb),
        in_specs=[pl.BlockSpec((tm, D), lambda i, j: (i, 0)),
                  pl.BlockSpec((tm, D), lambda i, j: (i, 0)),
                  pl.BlockSpec((PEER_HEADS, 4, N_KEYS, tm), lambda i, j: (0, 0, 0, i)),
                  pl.BlockSpec((PEER_HEADS, tm), lambda i, j: (0, i)),
                  pl.BlockSpec((eb, D), lambda i, j: (j, 0)),
                  pl.BlockSpec((D, eb), lambda i, j: (0, j))],
        out_specs=pl.BlockSpec((tm, D), lambda i, j: (i, 0)),
        out_shape=jax.ShapeDtypeStruct((T, D), f32),
        scratch_shapes=[pltpu.VMEM((D, tm), f32), pltpu.VMEM((eb, tm), bf16)],
        compiler_params=_params("parallel", "arbitrary"),
        name="peer_expert",
    )(x, hb, p, tau, u, vt)


def _final_norm_kernel(x_ref, g_ref, o_ref):
    o_ref[...] = _rms(x_ref[...], g_ref[...])


def _final_norm(x, g):
    T, D = x.shape
    tm = _largest_divisor(T, (512, 256, 128))
    return pl.pallas_call(
        _final_norm_kernel,
        grid=(T // tm,),
        in_specs=[pl.BlockSpec((tm, D), lambda i: (i, 0)), pl.BlockSpec((1, D), lambda i: (0, 0))],
        out_specs=pl.BlockSpec((tm, D), lambda i: (i, 0)),
        out_shape=jax.ShapeDtypeStruct((T, D), f32),
        compiler_params=_params("parallel"),
        name="final_norm",
    )(x, g.reshape(1, D))


def _trunk(x, w):
    B, S, D = x.shape
    T = B * S
    depth = w["g_mix"].shape[0]
    heads_a = D // (2 * HEAD_DIM)
    heads_b = D // (4 * HEAD_DIM)
    heads_c = D // HEAD_DIM
    kv_c = heads_c // 4
    cos, sin = _rope_tables_full(S)
    cos_ax, sin_ax = _rope_tables_axial(S)
    xf = x.reshape(T, D)
    for l in range(depth):
        if l % 2 == 0:
            e = l // 2
            lam_init = 0.8 - 0.6 * math.exp(-0.3 * l)
            proj = _norm_proj(xf, w["g_mix"][l], w["w_in_ab"][e], bf16).reshape(B, S, -1)
            oa = _attn_a(proj, cos, sin, heads_a)
            q_off = 3 * heads_a // 2
            ob = _attn_b(proj, w["lam"][e], w["g_subln"][e], cos, sin, lam_init,
                         q_off, q_off + heads_b, q_off + 2 * heads_b, heads_b)
            xf = _out_proj(xf, [oa.reshape(T, -1), ob.reshape(T, -1)], w["w_out_ab"][e])
        else:
            o = l // 2
            proj = _norm_proj(xf, w["g_mix"][l], w["w_qkv_c"][o], bf16).reshape(B, S, -1)
            oc = _attn_c(proj, w["g_qnorm"][o], w["g_knorm"][o], cos_ax, sin_ax, heads_c, kv_c)
            xf = _out_proj(xf, [oc.reshape(T, -1)], w["w_out_c"][o])
        q, hb = _norm_proj(xf, w["g_ffn"][l], w["w_peer_q"][l], f32, emit_h=True)
        p, tau = _peer_retrieve(q, w["peer_sub_keys"][l])
        xf = _peer_expert(xf, hb, p, tau, w["peer_u"][l], w["peer_vt"][l])
    return _final_norm(xf, w["g_final"]).reshape(B, S, D)


def kernel(x_prompt, x_sample, g_mix, g_ffn, g_final, w_in_ab, w_out_ab, lam_q1, lam_k1, lam_q2, lam_k2,
           g_subln, w_qkv_c, w_out_c, g_qnorm, g_knorm, w_peer_q, peer_sub_keys, peer_u, peer_v):
    w = dict(
        g_mix=g_mix, g_ffn=g_ffn, g_final=g_final,
        w_in_ab=w_in_ab.astype(bf16), w_out_ab=w_out_ab.astype(bf16),
        lam=jnp.stack([lam_q1, lam_k1, lam_q2, lam_k2], axis=1).astype(f32),
        g_subln=g_subln, w_qkv_c=w_qkv_c.astype(bf16), w_out_c=w_out_c.astype(bf16),
        g_qnorm=g_qnorm, g_knorm=g_knorm, w_peer_q=w_peer_q.astype(bf16),
        peer_sub_keys=peer_sub_keys, peer_u=peer_u.astype(bf16),
        peer_vt=jnp.swapaxes(peer_v.astype(bf16), 1, 2),
    )
    return (_trunk(x_prompt, w), _trunk(x_sample, w))
```

```python
import functools
import math

import jax
import jax.numpy as jnp
from jax import lax
from jax.experimental import pallas as pl
from jax.experimental.pallas import tpu as pltpu

f32 = jnp.float32
bf16 = jnp.bfloat16

HEAD_DIM = 128
LANES = 128
SUBLANES = 8
MXU_N = 256
ROPE_THETA = 10000.0
GRID_W = 64
EPS = 1e-6
NEG = -1e30
DIL_PAIRS = ((128, 1), (512, 4), (2048, 16))
PEER_HEADS = 8
N_KEYS = 128
PEER_TOPK = 16
VMEM_LIMIT = 60 * 1024 * 1024

_NT = (((1,), (1,)), ((), ()))


def _largest_divisor(n, candidates):
    for c in candidates:
        if n % c == 0:
            return c
    return n


def _params(*sem, flags=None):
    return pltpu.CompilerParams(dimension_semantics=sem, vmem_limit_bytes=VMEM_LIMIT, flags=flags)


def _rms(x, g):
    return x * lax.rsqrt(jnp.mean(x * x, axis=-1, keepdims=True) + EPS) * g


def _norm_proj_kernel(x_ref, g_ref, w_ref, *rest, emit_h):
    if emit_h:
        o_ref, hb_ref, h_scr = rest
    else:
        o_ref, h_scr = rest

    @pl.when(pl.program_id(1) == 0)
    def _():
        hb = _rms(x_ref[...], g_ref[...]).astype(bf16)
        h_scr[...] = hb
        if emit_h:
            hb_ref[...] = hb

    o_ref[...] = jnp.dot(h_scr[...], w_ref[...], preferred_element_type=f32).astype(o_ref.dtype)


def _norm_proj(x, g, w, out_dtype, emit_h=False):
    T, D = x.shape
    N = w.shape[1]
    tm = _largest_divisor(T, (1024, 512, 256, 128))
    tn = _largest_divisor(N, (512, 256, 128))
    out_shape = [jax.ShapeDtypeStruct((T, N), out_dtype)]
    out_specs = [pl.BlockSpec((tm, tn), lambda i, j: (i, j))]
    if emit_h:
        out_shape.append(jax.ShapeDtypeStruct((T, D), bf16))
        out_specs.append(pl.BlockSpec((tm, D), lambda i, j: (i, 0)))
    res = pl.pallas_call(
        functools.partial(_norm_proj_kernel, emit_h=emit_h),
        grid=(T // tm, N // tn),
        in_specs=[pl.BlockSpec((tm, D), lambda i, j: (i, 0)),
                  pl.BlockSpec((1, D), lambda i, j: (0, 0)),
                  pl.BlockSpec((D, tn), lambda i, j: (0, j))],
        out_specs=out_specs,
        out_shape=out_shape,
        scratch_shapes=[pltpu.VMEM((tm, D), bf16)],
        compiler_params=_params("parallel", "arbitrary"),
        name="norm_proj",
    )(x, g.reshape(1, D), w)
    return res if emit_h else res[0]


def _out_proj_kernel(*refs, n_in):
    res_ref = refs[0]
    a_refs = refs[1:1 + n_in]
    w_refs = refs[1 + n_in:1 + 2 * n_in]
    o_ref = refs[1 + 2 * n_in]
    acc = res_ref[...]
    for a_ref, w_ref in zip(a_refs, w_refs):
        acc = acc + jnp.dot(a_ref[...], w_ref[...], preferred_element_type=f32)
    o_ref[...] = acc


def _out_proj(res, acts, w):
    T, N = res.shape
    tm = _largest_divisor(T, (1024, 512, 256, 128))
    tn = _largest_divisor(N, (512, 256, 128))
    n_in = len(acts)
    K = acts[0].shape[1]
    assert all(a.shape[1] == K for a in acts) and w.shape[0] == n_in * K
    in_specs = [pl.BlockSpec((tm, tn), lambda i, j: (i, j))]
    in_specs += [pl.BlockSpec((tm, K), lambda i, j: (i, 0)) for _ in acts]
    in_specs += [pl.BlockSpec((K, tn), functools.partial(lambda i, j, kk: (kk, j), kk=kk))
                 for kk in range(n_in)]
    return pl.pallas_call(
        functools.partial(_out_proj_kernel, n_in=n_in),
        grid=(T // tm, N // tn),
        in_specs=in_specs,
        out_specs=pl.BlockSpec((tm, tn), lambda i, j: (i, j)),
        out_shape=jax.ShapeDtypeStruct((T, N), f32),
        compiler_params=_params("parallel", "parallel"),
        name="out_proj",
    )(res, *acts, *([w] * n_in))


def _rope_tables(pos, dim):
    inv = ROPE_THETA ** (-jnp.arange(0, dim, 2, dtype=f32) / dim)
    ang = pos[:, None] * inv[None, :]
    ang = jnp.concatenate([ang, ang], axis=-1)
    return jnp.cos(ang), jnp.sin(ang)


def _rope_tables_full(S):
    cos, sin = _rope_tables(jnp.arange(S, dtype=f32), HEAD_DIM)
    half = HEAD_DIM // 2
    sign = jnp.where(jnp.arange(HEAD_DIM) < half, -1.0, 1.0).astype(f32)
    return cos, sin * sign[None, :]


def _rope_tables_axial(S):
    rows = S // GRID_W
    row_idx = jnp.repeat(jnp.arange(rows, dtype=f32), GRID_W)
    col_idx = jnp.tile(jnp.arange(GRID_W, dtype=f32), rows)
    cr, sr = _rope_tables(row_idx, HEAD_DIM // 2)
    cc, sc = _rope_tables(col_idx, HEAD_DIM // 2)
    quarter = HEAD_DIM // 4
    sign = jnp.where(jnp.arange(HEAD_DIM // 2) < quarter, -1.0, 1.0).astype(f32)
    cos = jnp.concatenate([cr, cc], axis=-1)
    sin = jnp.concatenate([sr * sign[None, :], sc * sign[None, :]], axis=-1)
    return cos, sin


def _rope(x, cos, sin_signed):
    return x * cos + pltpu.roll(x, HEAD_DIM // 2, axis=1) * sin_signed


def _rope_axial(x, cos, sin_signed):
    quarter = HEAD_DIM // 4
    lane = lax.broadcasted_iota(jnp.int32, x.shape, 1)
    lower = (lane % (2 * quarter)) < quarter
    rot = jnp.where(lower, pltpu.roll(x, HEAD_DIM - quarter, axis=1), pltpu.roll(x, quarter, axis=1))
    return x * cos + rot * sin_signed


def _softmax_pv(s, v):
    m = jnp.max(s, axis=-1, keepdims=True)
    p = jnp.exp(s - m)
    l = jnp.sum(p, axis=-1, keepdims=True)
    return jnp.dot(p.astype(bf16), v, preferred_element_type=f32) / l


def _attn_c_kernel(q_ref, k_ref, v_ref, cos_ref, sin_ref, cosq_ref, sinq_ref, gq_ref, gk_ref,
                   o_ref, k_scr, *, groups, scale):
    @pl.when(pl.program_id(2) == 0)
    def _():
        k = _rms(k_ref[0].astype(f32), gk_ref[...])
        k_scr[...] = _rope_axial(k, cos_ref[...], sin_ref[...]).astype(bf16)

    cq = cosq_ref[...]
    sq = sinq_ref[...]
    gq = gq_ref[...]
    v = v_ref[0]
    for g in range(groups):
        cols = slice(g * HEAD_DIM, (g + 1) * HEAD_DIM)
        q = _rms(q_ref[0, :, cols].astype(f32), gq)
        q = _rope_axial(q, cq, sq).astype(bf16)
        s = lax.dot_general(q, k_scr[...], _NT, preferred_element_type=f32) * scale
        o_ref[0, :, cols] = _softmax_pv(s, v).astype(o_ref.dtype)


def _attn_c(proj, g_q, g_k, cos, sin, n_heads, n_kv):
    B, S, _ = proj.shape
    groups = n_heads // n_kv
    tq = _largest_divisor(S, (256, 128))
    qw = groups * HEAD_DIM
    k_off = n_heads
    v_off = n_heads + n_kv
    return pl.pallas_call(
        functools.partial(_attn_c_kernel, groups=groups, scale=HEAD_DIM ** -0.5),
        grid=(B, n_kv, S // tq),
        in_specs=[pl.BlockSpec((1, tq, qw), lambda b, h, i: (b, i, h)),
                  pl.BlockSpec((1, S, HEAD_DIM), lambda b, h, i: (b, 0, k_off + h)),
                  pl.BlockSpec((1, S, HEAD_DIM), lambda b, h, i: (b, 0, v_off + h)),
                  pl.BlockSpec((S, HEAD_DIM), lambda b, h, i: (0, 0)),
                  pl.BlockSpec((S, HEAD_DIM), lambda b, h, i: (0, 0)),
                  pl.BlockSpec((tq, HEAD_DIM), lambda b, h, i: (i, 0)),
                  pl.BlockSpec((tq, HEAD_DIM), lambda b, h, i: (i, 0)),
                  pl.BlockSpec((1, HEAD_DIM), lambda b, h, i: (0, 0)),
                  pl.BlockSpec((1, HEAD_DIM), lambda b, h, i: (0, 0))],
        out_specs=pl.BlockSpec((1, tq, qw), lambda b, h, i: (b, i, h)),
        out_shape=jax.ShapeDtypeStruct((B, S, n_heads * HEAD_DIM), bf16),
        scratch_shapes=[pltpu.VMEM((S, HEAD_DIM), bf16)],
        compiler_params=_params("parallel", "parallel", "arbitrary"),
        name="attn_c",
    )(proj, proj, proj, cos, sin, cos, sin, g_q.reshape(1, HEAD_DIM), g_k.reshape(1, HEAD_DIM))


def _attn_b_kernel(lam_ref, q_ref, k_ref, v_ref, cos_ref, sin_ref, cosq_ref, sinq_ref, gsub_ref,
                   o_ref, k_scr, *, scale, lam_init):
    @pl.when(pl.program_id(2) == 0)
    def _():
        for c in range(2):
            k = k_ref[0, :, c * HEAD_DIM:(c + 1) * HEAD_DIM].astype(f32)
            k_scr[c] = _rope(k, cos_ref[...], sin_ref[...]).astype(bf16)

    lam_p = lam_ref[...]
    lam = (jnp.exp(jnp.sum(lam_p[0:1] * lam_p[1:2], axis=-1, keepdims=True))
           - jnp.exp(jnp.sum(lam_p[2:3] * lam_p[3:4], axis=-1, keepdims=True)) + lam_init)
    cq = cosq_ref[...]
    sq = sinq_ref[...]
    v = v_ref[0]
    outs = []
    for c in range(2):
        q = q_ref[0, :, c * HEAD_DIM:(c + 1) * HEAD_DIM].astype(f32)
        q = _rope(q, cq, sq).astype(bf16)
        s = lax.dot_general(q, k_scr[c], _NT, preferred_element_type=f32) * scale
        outs.append(_softmax_pv(s, v))
    o = outs[0] - lam * outs[1]
    o = _rms(o, gsub_ref[...]) * (1.0 - lam_init)
    o_ref[0] = o.astype(o_ref.dtype)


def _attn_b(proj, lam_params, g_sub, cos, sin, lam_init, q_off, k_off, v_off, n_heads):
    B, S, _ = proj.shape
    tq = _largest_divisor(S, (256, 128))
    hw = 2 * HEAD_DIM
    return pl.pallas_call(
        functools.partial(_attn_b_kernel, scale=HEAD_DIM ** -0.5, lam_init=lam_init),
        grid=(B, n_heads, S // tq),
        in_specs=[pl.BlockSpec((4, HEAD_DIM), lambda b, h, i: (0, 0)),
                  pl.BlockSpec((1, tq, hw), lambda b, h, i: (b, i, q_off + h)),
                  pl.BlockSpec((1, S, hw), lambda b, h, i: (b, 0, k_off + h)),
                  pl.BlockSpec((1, S, hw), lambda b, h, i: (b, 0, v_off + h)),
                  pl.BlockSpec((S, HEAD_DIM), lambda b, h, i: (0, 0)),
                  pl.BlockSpec((S, HEAD_DIM), lambda b, h, i: (0, 0)),
                  pl.BlockSpec((tq, HEAD_DIM), lambda b, h, i: (i, 0)),
                  pl.BlockSpec((tq, HEAD_DIM), lambda b, h, i: (i, 0)),
                  pl.BlockSpec((1, hw), lambda b, h, i: (0, 0))],
        out_specs=pl.BlockSpec((1, tq, hw), lambda b, h, i: (b, i, h)),
        out_shape=jax.ShapeDtypeStruct((B, S, n_heads * hw), bf16),
        scratch_shapes=[pltpu.VMEM((2, S, HEAD_DIM), bf16)],
        compiler_params=_params("parallel", "parallel", "arbitrary"),
        name="attn_b",
    )(lam_params, proj, proj, proj, cos, sin, cos, sin, g_sub.reshape(1, hw))


def _attn_a_kernel(q_ref, k_ref, v_ref, cos_ref, sin_ref, o_ref, qf, kf, vf, *scr, S, scale):
    nb = len(DIL_PAIRS)
    o_scr, l_scr = scr[:nb], scr[nb:]
    cos = cos_ref[...]
    sin = sin_ref[...]
    qf[...] = _rope(q_ref[0].astype(f32), cos, sin)
    kf[...] = _rope(k_ref[0].astype(f32), cos, sin)
    vf[...] = v_ref[0].astype(f32)

    for bi, (window, dil) in enumerate(DIL_PAIRS):
        L = S // dil
        half = window // (2 * dil)
        qc_n = min(128, L)
        kw_n = min(L, qc_n + 2 * half)
        n_chunks = L // qc_n

        def rows(start, n, dil=dil):
            return pl.ds(start, n) if dil == 1 else pl.ds(start, n, stride=dil)

        def body(idx, carry, dil=dil, half=half, qc_n=qc_n, kw_n=kw_n, n_chunks=n_chunks, L=L,
                 bi=bi, rows=rows):
            r = idx // n_chunks
            q0 = (idx % n_chunks) * qc_n
            k0 = jnp.clip(q0 - half, 0, L - kw_n)
            q_rows = rows(r + dil * q0, qc_n)
            k_rows = rows(r + dil * k0, kw_n)
            qc = qf[q_rows, :].astype(bf16)
            kc = kf[k_rows, :].astype(bf16)
            vc = vf[k_rows, :].astype(bf16)
            s = lax.dot_general(qc, kc, _NT, preferred_element_type=f32) * scale
            qpos = q0 + lax.broadcasted_iota(jnp.int32, (qc_n, kw_n), 0)
            kpos = k0 + lax.broadcasted_iota(jnp.int32, (qc_n, kw_n), 1)
            s = jnp.where(jnp.abs(kpos - qpos) <= half, s, NEG)
            m = jnp.max(s, axis=-1, keepdims=True)
            p = jnp.exp(s - m)
            den = jnp.sum(p, axis=-1, keepdims=True)
            o_scr[bi][q_rows, :] = jnp.dot((p / den).astype(bf16), vc, preferred_element_type=f32)
            l_scr[bi][q_rows, :] = jnp.broadcast_to(m + jnp.log(den), (qc_n, HEAD_DIM))
            return carry

        lax.fori_loop(0, dil * n_chunks, body, 0)

    lses = [l_scr[bi][...] for bi in range(nb)]
    m = functools.reduce(jnp.maximum, lses)
    es = [jnp.exp(l - m) for l in lses]
    z = functools.reduce(lambda a, b: a + b, es)
    out = functools.reduce(lambda a, b: a + b, [(e / z) * o_scr[bi][...] for bi, e in enumerate(es)])
    o_ref[0] = out.astype(o_ref.dtype)


def _attn_a(proj, cos, sin, n_heads):
    B, S, _ = proj.shape
    nb = len(DIL_PAIRS)
    blk = lambda off: pl.BlockSpec((1, S, HEAD_DIM), lambda b, h: (b, 0, off + h))
    tab = pl.BlockSpec((S, HEAD_DIM), lambda b, h: (0, 0))
    return pl.pallas_call(
        functools.partial(_attn_a_kernel, S=S, scale=HEAD_DIM ** -0.5),
        grid=(B, n_heads),
        in_specs=[blk(0), blk(n_heads), blk(2 * n_heads), tab, tab],
        out_specs=pl.BlockSpec((1, S, HEAD_DIM), lambda b, h: (b, 0, h)),
        out_shape=jax.ShapeDtypeStruct((B, S, n_heads * HEAD_DIM), bf16),
        scratch_shapes=[pltpu.VMEM((S, HEAD_DIM), f32), pltpu.VMEM((S, HEAD_DIM), f32),
                        pltpu.VMEM((S, HEAD_DIM), f32)]
                       + [pltpu.VMEM((S, HEAD_DIM), f32) for _ in range(2 * nb)],
        compiler_params=_params("parallel", "parallel"),
        name="attn_a",
    )(proj, proj, proj, cos, sin)


def _topk_vals(s, k):
    n_rows = s.shape[0]
    rows = lax.broadcasted_iota(jnp.int32, s.shape, 0)
    out = []
    for _ in range(k):
        m = jnp.max(s, axis=0, keepdims=True)
        first = jnp.min(jnp.where(s == m, rows, n_rows), axis=0, keepdims=True)
        s = jnp.where(rows == first, -jnp.inf, s)
        out.append(m)
    return jnp.concatenate(out, axis=0)


def _peer_retrieve_kernel(q_ref, keys_ref, p1_ref, p2_ref, tau_ref):
    K = PEER_TOPK
    scores, tops = [], []
    for c in range(2):
        qc = q_ref[:, c * N_KEYS:(c + 1) * N_KEYS].astype(bf16)
        kc = keys_ref[0, c].astype(bf16)
        s = lax.dot_general(kc, qc, _NT, preferred_element_type=f32)
        scores.append(s)
        tops.append(_topk_vals(s, K))
    v1, v2 = tops
    cand = [v1 + v2[0:1]]
    cand += [v1[0:8] + v2[j:j + 1] for j in range(1, 8)]
    cand += [v1[0:1] + v2[8:16]]
    top = _topk_vals(jnp.concatenate(cand, axis=0), K)
    tau = top[K - 1:K]
    z = jnp.sum(jnp.exp(top - top[0:1]), axis=0, keepdims=True)
    p1_ref[0, 0] = scores[0]
    p1_ref[0, 1] = jnp.exp(scores[0] - v1[0:1]) / z
    p2_ref[0, 0] = scores[1]
    p2_ref[0, 1] = jnp.exp(scores[1] - v2[0:1])
    tau_ref[pl.ds(pl.program_id(1), 1), :] = tau


def _peer_retrieve(q, sub_keys):
    T = q.shape[0]
    tm = _largest_divisor(T, (256, 128))
    p_spec = pl.BlockSpec((1, 2, N_KEYS, tm), lambda i, h: (h, 0, 0, i))
    p_shape = jax.ShapeDtypeStruct((PEER_HEADS, 2, N_KEYS, T), f32)
    return pl.pallas_call(
        _peer_retrieve_kernel,
        grid=(T // tm, PEER_HEADS),
        in_specs=[pl.BlockSpec((tm, 2 * N_KEYS), lambda i, h: (i, h)),
                  pl.BlockSpec((1, 2, N_KEYS, HEAD_DIM), lambda i, h: (h, 0, 0, 0))],
        out_specs=[p_spec, p_spec, pl.BlockSpec((PEER_HEADS, tm), lambda i, h: (0, i))],
        out_shape=[p_shape, p_shape, jax.ShapeDtypeStruct((PEER_HEADS, T), f32)],
        compiler_params=_params("parallel", "arbitrary"),
        name="peer_retrieve",
    )(q, sub_keys)


PEER_EB = 1024
GATE_ROWS = 16


def _gelu(x):
    return 0.5 * x * (1.0 + lax.erf(x * (1.0 / math.sqrt(2.0))))


def _peer_expert_kernel(x_ref, hb_ref, p1_ref, p2_ref, tau_ref, u_ref, vt_ref, o_ref, acc_ref, at_ref):
    j = pl.program_id(1)
    eb, tm = at_ref.shape
    heads = range(PEER_HEADS)
    group = 2 * N_KEYS

    @pl.when(j == 0)
    def _():
        acc_ref[...] = jnp.zeros_like(acc_ref)

    def body(g, carry):
        u_rows = u_ref[pl.ds(pl.multiple_of(g * group, group), group), :]
        s1_rows = [[p1_ref[h, 0, pl.ds(g * (group // N_KEYS) + aa, 1), :] for h in heads]
                   for aa in range(group // N_KEYS)]
        e1_rows = [[p1_ref[h, 1, pl.ds(g * (group // N_KEYS) + aa, 1), :] for h in heads]
                   for aa in range(group // N_KEYS)]
        for t0 in range(0, tm, MXU_N):
            ht = lax.dot_general(u_rows, hb_ref[t0:t0 + MXU_N, :], _NT, preferred_element_type=f32)
            for aa in range(group // N_KEYS):
                sb = g * (group // N_KEYS) + aa
                for l0 in range(t0, t0 + MXU_N, LANES):
                    lanes = slice(l0, l0 + LANES)
                    tau = [tau_ref[h:h + 1, lanes] for h in heads]
                    s1 = [row[:, lanes] for row in s1_rows[aa]]
                    e1 = [row[:, lanes] for row in e1_rows[aa]]
                    for r in range(0, N_KEYS, GATE_ROWS):
                        krows = slice(r, r + GATE_ROWS)
                        gate = None
                        for h in heads:
                            hit = (s1[h] + p2_ref[h, 0, krows, lanes]) >= tau[h]
                            w = jnp.where(hit, e1[h] * p2_ref[h, 1, krows, lanes], 0.0)
                            gate = w if gate is None else gate + w
                        hrows = slice(aa * N_KEYS + r, aa * N_KEYS + r + GATE_ROWS)
                        erows = pl.ds(pl.multiple_of(sb * N_KEYS + r, GATE_ROWS), GATE_ROWS)
                        act = gate * _gelu(ht[hrows, l0 - t0:l0 - t0 + LANES])
                        at_ref[erows, lanes] = act.astype(bf16)
        return carry

    lax.fori_loop(0, eb // group, body, 0)
    acc_ref[...] += jnp.dot(vt_ref[...], at_ref[...], preferred_element_type=f32)

    @pl.when(j == pl.num_programs(1) - 1)
    def _():
        o_ref[...] = x_ref[...] + acc_ref[...].T


def _peer_expert(x, hb, p1, p2, tau, u, vt):
    T, D = x.shape
    E = u.shape[0]
    tm = _largest_divisor(T, (512, 256, 128))
    eb = PEER_EB
    once = dict(pipeline_mode=pl.Buffered(1))
    return pl.pallas_call(
        _peer_expert_kernel,
        grid=(T // tm, E // eb),
        in_specs=[pl.BlockSpec((tm, D), lambda i, j: (i, 0), **once),
                  pl.BlockSpec((tm, D), lambda i, j: (i, 0), **once),
                  pl.BlockSpec((PEER_HEADS, 2, eb // N_KEYS, tm), lambda i, j: (0, 0, j, i)),
                  pl.BlockSpec((PEER_HEADS, 2, N_KEYS, tm), lambda i, j: (0, 0, 0, i), **once),
                  pl.BlockSpec((PEER_HEADS, tm), lambda i, j: (0, i)),
                  pl.BlockSpec((eb, D), lambda i, j: (j, 0)),
                  pl.BlockSpec((D, eb), lambda i, j: (0, j))],
        out_specs=pl.BlockSpec((tm, D), lambda i, j: (i, 0)),
        out_shape=jax.ShapeDtypeStruct((T, D), f32),
        scratch_shapes=[pltpu.VMEM((D, tm), f32), pltpu.VMEM((eb, tm), bf16)],
        compiler_params=_params("parallel", "arbitrary"),
        name="peer_expert",
    )(x, hb, p1, p2, tau, u, vt)


def _final_norm_kernel(x_ref, g_ref, o_ref):
    o_ref[...] = _rms(x_ref[...], g_ref[...])


def _final_norm(x, g):
    T, D = x.shape
    tm = _largest_divisor(T, (512, 256, 128))
    return pl.pallas_call(
        _final_norm_kernel,
        grid=(T // tm,),
        in_specs=[pl.BlockSpec((tm, D), lambda i: (i, 0)), pl.BlockSpec((1, D), lambda i: (0, 0))],
        out_specs=pl.BlockSpec((tm, D), lambda i: (i, 0)),
        out_shape=jax.ShapeDtypeStruct((T, D), f32),
        compiler_params=_params("parallel"),
        name="final_norm",
    )(x, g.reshape(1, D))


def _trunk(x, w):
    B, S, D = x.shape
    T = B * S
    depth = w["g_mix"].shape[0]
    heads_a = D // (2 * HEAD_DIM)
    heads_b = D // (4 * HEAD_DIM)
    heads_c = D // HEAD_DIM
    kv_c = heads_c // 4
    cos, sin = _rope_tables_full(S)
    cos_ax, sin_ax = _rope_tables_axial(S)
    xf = x.reshape(T, D)
    for l in range(depth):
        if l % 2 == 0:
            e = l // 2
            lam_init = 0.8 - 0.6 * math.exp(-0.3 * l)
            proj = _norm_proj(xf, w["g_mix"][l], w["w_in_ab"][e], bf16).reshape(B, S, -1)
            oa = _attn_a(proj, cos, sin, heads_a)
            q_off = 3 * heads_a // 2
            ob = _attn_b(proj, w["lam"][e], w["g_subln"][e], cos, sin, lam_init,
                         q_off, q_off + heads_b, q_off + 2 * heads_b, heads_b)
            xf = _out_proj(xf, [oa.reshape(T, -1), ob.reshape(T, -1)], w["w_out_ab"][e])
        else:
            o = l // 2
            proj = _norm_proj(xf, w["g_mix"][l], w["w_qkv_c"][o], bf16).reshape(B, S, -1)
            oc = _attn_c(proj, w["g_qnorm"][o], w["g_knorm"][o], cos_ax, sin_ax, heads_c, kv_c)
            xf = _out_proj(xf, [oc.reshape(T, -1)], w["w_out_c"][o])
        q, hb = _norm_proj(xf, w["g_ffn"][l], w["w_peer_q"][l], f32, emit_h=True)
        p1, p2, tau = _peer_retrieve(q, w["peer_sub_keys"][l])
        xf = _peer_expert(xf, hb, p1, p2, tau, w["peer_u"][l], w["peer_vt"][l])
    return _final_norm(xf, w["g_final"]).reshape(B, S, D)


def kernel(x_prompt, x_sample, g_mix, g_ffn, g_final, w_in_ab, w_out_ab, lam_q1, lam_k1, lam_q2, lam_k2,
           g_subln, w_qkv_c, w_out_c, g_qnorm, g_knorm, w_peer_q, peer_sub_keys, peer_u, peer_v):
    w = dict(
        g_mix=g_mix, g_ffn=g_ffn, g_final=g_final,
        w_in_ab=w_in_ab.astype(bf16), w_out_ab=w_out_ab.astype(bf16),
        lam=jnp.stack([lam_q1, lam_k1, lam_q2, lam_k2], axis=1).astype(f32),
        g_subln=g_subln, w_qkv_c=w_qkv_c.astype(bf16), w_out_c=w_out_c.astype(bf16),
        g_qnorm=g_qnorm, g_knorm=g_knorm, w_peer_q=w_peer_q.astype(bf16),
        peer_sub_keys=peer_sub_keys, peer_u=peer_u.astype(bf16),
        peer_vt=jnp.swapaxes(peer_v.astype(bf16), 1, 2),
    )
    return (_trunk(x_prompt, w), _trunk(x_sample, w))
```

```python
import functools
import math

import jax
import jax.numpy as jnp
from jax import lax
from jax.experimental import pallas as pl
from jax.experimental.pallas import tpu as pltpu

f32 = jnp.float32
bf16 = jnp.bfloat16

HEAD_DIM = 128
LANES = 128
SUBLANES = 8
MXU_N = 256
ROPE_THETA = 10000.0
GRID_W = 64
EPS = 1e-6
NEG = -1e30
DIL_PAIRS = ((128, 1), (512, 4), (2048, 16))
CHUNK_UNROLL = 4
PEER_HEADS = 8
N_KEYS = 128
PEER_TOPK = 16
VMEM_LIMIT = 60 * 1024 * 1024

_NT = (((1,), (1,)), ((), ()))


def _largest_divisor(n, candidates):
    for c in candidates:
        if n % c == 0:
            return c
    return n


def _params(*sem, flags=None):
    return pltpu.CompilerParams(dimension_semantics=sem, vmem_limit_bytes=VMEM_LIMIT, flags=flags)


def _rms(x, g):
    return x * lax.rsqrt(jnp.mean(x * x, axis=-1, keepdims=True) + EPS) * g


def _norm_proj_kernel(x_ref, g_ref, w_ref, *rest, emit_h):
    if emit_h:
        o_ref, hb_ref, h_scr = rest
    else:
        o_ref, h_scr = rest

    @pl.when(pl.program_id(1) == 0)
    def _():
        hb = _rms(x_ref[...], g_ref[...]).astype(bf16)
        h_scr[...] = hb
        if emit_h:
            hb_ref[...] = hb

    o_ref[...] = jnp.dot(h_scr[...], w_ref[...], preferred_element_type=f32).astype(o_ref.dtype)


def _norm_proj(x, g, w, out_dtype, emit_h=False):
    T, D = x.shape
    N = w.shape[1]
    tm = _largest_divisor(T, (1024, 512, 256, 128))
    tn = _largest_divisor(N, (1024, 512, 256, 128))
    out_shape = [jax.ShapeDtypeStruct((T, N), out_dtype)]
    out_specs = [pl.BlockSpec((tm, tn), lambda i, j: (i, j))]
    if emit_h:
        out_shape.append(jax.ShapeDtypeStruct((T, D), bf16))
        out_specs.append(pl.BlockSpec((tm, D), lambda i, j: (i, 0)))
    res = pl.pallas_call(
        functools.partial(_norm_proj_kernel, emit_h=emit_h),
        grid=(T // tm, N // tn),
        in_specs=[pl.BlockSpec((tm, D), lambda i, j: (i, 0)),
                  pl.BlockSpec((1, D), lambda i, j: (0, 0)),
                  pl.BlockSpec((D, tn), lambda i, j: (0, j))],
        out_specs=out_specs,
        out_shape=out_shape,
        scratch_shapes=[pltpu.VMEM((tm, D), bf16)],
        compiler_params=_params("parallel", "arbitrary"),
        name="norm_proj",
    )(x, g.reshape(1, D), w)
    return res if emit_h else res[0]


def _out_proj_kernel(*refs, n_in):
    res_ref = refs[0]
    a_refs = refs[1:1 + n_in]
    w_refs = refs[1 + n_in:1 + 2 * n_in]
    o_ref = refs[1 + 2 * n_in]
    acc = res_ref[...]
    for a_ref, w_ref in zip(a_refs, w_refs):
        acc = acc + jnp.dot(a_ref[...], w_ref[...], preferred_element_type=f32)
    o_ref[...] = acc


def _out_proj(res, acts, w):
    T, N = res.shape
    tm = _largest_divisor(T, (1024, 512, 256, 128))
    tn = _largest_divisor(N, (1024, 512, 256, 128))
    n_in = len(acts)
    K = acts[0].shape[1]
    assert all(a.shape[1] == K for a in acts) and w.shape[0] == n_in * K
    in_specs = [pl.BlockSpec((tm, tn), lambda i, j: (i, j))]
    in_specs += [pl.BlockSpec((tm, K), lambda i, j: (i, 0)) for _ in acts]
    in_specs += [pl.BlockSpec((K, tn), functools.partial(lambda i, j, kk: (kk, j), kk=kk))
                 for kk in range(n_in)]
    return pl.pallas_call(
        functools.partial(_out_proj_kernel, n_in=n_in),
        grid=(T // tm, N // tn),
        in_specs=in_specs,
        out_specs=pl.BlockSpec((tm, tn), lambda i, j: (i, j)),
        out_shape=jax.ShapeDtypeStruct((T, N), f32),
        compiler_params=_params("parallel", "parallel"),
        name="out_proj",
    )(res, *acts, *([w] * n_in))


def _rope_tables(pos, dim):
    inv = ROPE_THETA ** (-jnp.arange(0, dim, 2, dtype=f32) / dim)
    ang = pos[:, None] * inv[None, :]
    ang = jnp.concatenate([ang, ang], axis=-1)
    return jnp.cos(ang), jnp.sin(ang)


def _rope_tables_full(S):
    cos, sin = _rope_tables(jnp.arange(S, dtype=f32), HEAD_DIM)
    half = HEAD_DIM // 2
    sign = jnp.where(jnp.arange(HEAD_DIM) < half, -1.0, 1.0).astype(f32)
    return cos, sin * sign[None, :]


def _rope_tables_axial(S):
    rows = S // GRID_W
    row_idx = jnp.repeat(jnp.arange(rows, dtype=f32), GRID_W)
    col_idx = jnp.tile(jnp.arange(GRID_W, dtype=f32), rows)
    cr, sr = _rope_tables(row_idx, HEAD_DIM // 2)
    cc, sc = _rope_tables(col_idx, HEAD_DIM // 2)
    quarter = HEAD_DIM // 4
    sign = jnp.where(jnp.arange(HEAD_DIM // 2) < quarter, -1.0, 1.0).astype(f32)
    cos = jnp.concatenate([cr, cc], axis=-1)
    sin = jnp.concatenate([sr * sign[None, :], sc * sign[None, :]], axis=-1)
    return cos, sin


def _rope(x, cos, sin_signed):
    return x * cos + pltpu.roll(x, HEAD_DIM // 2, axis=1) * sin_signed


def _rope_axial(x, cos, sin_signed):
    quarter = HEAD_DIM // 4
    lane = lax.broadcasted_iota(jnp.int32, x.shape, 1)
    lower = (lane % (2 * quarter)) < quarter
    rot = jnp.where(lower, pltpu.roll(x, HEAD_DIM - quarter, axis=1), pltpu.roll(x, quarter, axis=1))
    return x * cos + rot * sin_signed


def _softmax_pv(s, v):
    m = jnp.max(s, axis=-1, keepdims=True)
    p = jnp.exp(s - m)
    l = jnp.sum(p, axis=-1, keepdims=True)
    return jnp.dot(p.astype(bf16), v, preferred_element_type=f32) / l


def _attn_c_kernel(q_ref, k_ref, v_ref, cos_ref, sin_ref, cosq_ref, sinq_ref, gq_ref, gk_ref,
                   o_ref, k_scr, *, groups, scale):
    @pl.when(pl.program_id(2) == 0)
    def _():
        k = _rms(k_ref[0].astype(f32), gk_ref[...])
        k_scr[...] = _rope_axial(k, cos_ref[...], sin_ref[...]).astype(bf16)

    cq = cosq_ref[...]
    sq = sinq_ref[...]
    gq = gq_ref[...]
    v = v_ref[0]
    for g in range(groups):
        cols = slice(g * HEAD_DIM, (g + 1) * HEAD_DIM)
        q = _rms(q_ref[0, :, cols].astype(f32), gq)
        q = _rope_axial(q, cq, sq).astype(bf16)
        s = lax.dot_general(q, k_scr[...], _NT, preferred_element_type=f32) * scale
        o_ref[0, :, cols] = _softmax_pv(s, v).astype(o_ref.dtype)


def _attn_c(proj, g_q, g_k, cos, sin, n_heads, n_kv):
    B, S, _ = proj.shape
    groups = n_heads // n_kv
    tq = _largest_divisor(S, (256, 128))
    qw = groups * HEAD_DIM
    k_off = n_heads
    v_off = n_heads + n_kv
    return pl.pallas_call(
        functools.partial(_attn_c_kernel, groups=groups, scale=HEAD_DIM ** -0.5),
        grid=(B, n_kv, S // tq),
        in_specs=[pl.BlockSpec((1, tq, qw), lambda b, h, i: (b, i, h)),
                  pl.BlockSpec((1, S, HEAD_DIM), lambda b, h, i: (b, 0, k_off + h)),
                  pl.BlockSpec((1, S, HEAD_DIM), lambda b, h, i: (b, 0, v_off + h)),
                  pl.BlockSpec((S, HEAD_DIM), lambda b, h, i: (0, 0)),
                  pl.BlockSpec((S, HEAD_DIM), lambda b, h, i: (0, 0)),
                  pl.BlockSpec((tq, HEAD_DIM), lambda b, h, i: (i, 0)),
                  pl.BlockSpec((tq, HEAD_DIM), lambda b, h, i: (i, 0)),
                  pl.BlockSpec((1, HEAD_DIM), lambda b, h, i: (0, 0)),
                  pl.BlockSpec((1, HEAD_DIM), lambda b, h, i: (0, 0))],
        out_specs=pl.BlockSpec((1, tq, qw), lambda b, h, i: (b, i, h)),
        out_shape=jax.ShapeDtypeStruct((B, S, n_heads * HEAD_DIM), bf16),
        scratch_shapes=[pltpu.VMEM((S, HEAD_DIM), bf16)],
        compiler_params=_params("parallel", "parallel", "arbitrary"),
        name="attn_c",
    )(proj, proj, proj, cos, sin, cos, sin, g_q.reshape(1, HEAD_DIM), g_k.reshape(1, HEAD_DIM))


def _attn_b_kernel(lam_ref, q_ref, k_ref, v_ref, cos_ref, sin_ref, cosq_ref, sinq_ref, gsub_ref,
                   o_ref, k_scr, *, scale, lam_init):
    @pl.when(pl.program_id(2) == 0)
    def _():
        for c in range(2):
            k = k_ref[0, :, c * HEAD_DIM:(c + 1) * HEAD_DIM].astype(f32)
            k_scr[c] = _rope(k, cos_ref[...], sin_ref[...]).astype(bf16)

    lam_p = lam_ref[...]
    lam = (jnp.exp(jnp.sum(lam_p[0:1] * lam_p[1:2], axis=-1, keepdims=True))
           - jnp.exp(jnp.sum(lam_p[2:3] * lam_p[3:4], axis=-1, keepdims=True)) + lam_init)
    cq = cosq_ref[...]
    sq = sinq_ref[...]
    v = v_ref[0]
    outs = []
    for c in range(2):
        q = q_ref[0, :, c * HEAD_DIM:(c + 1) * HEAD_DIM].astype(f32)
        q = _rope(q, cq, sq).astype(bf16)
        s = lax.dot_general(q, k_scr[c], _NT, preferred_element_type=f32) * scale
        outs.append(_softmax_pv(s, v))
    o = outs[0] - lam * outs[1]
    o = _rms(o, gsub_ref[...]) * (1.0 - lam_init)
    o_ref[0] = o.astype(o_ref.dtype)


def _attn_b(proj, lam_params, g_sub, cos, sin, lam_init, q_off, k_off, v_off, n_heads):
    B, S, _ = proj.shape
    tq = _largest_divisor(S, (256, 128))
    hw = 2 * HEAD_DIM
    return pl.pallas_call(
        functools.partial(_attn_b_kernel, scale=HEAD_DIM ** -0.5, lam_init=lam_init),
        grid=(B, n_heads, S // tq),
        in_specs=[pl.BlockSpec((4, HEAD_DIM), lambda b, h, i: (0, 0)),
                  pl.BlockSpec((1, tq, hw), lambda b, h, i: (b, i, q_off + h)),
                  pl.BlockSpec((1, S, hw), lambda b, h, i: (b, 0, k_off + h)),
                  pl.BlockSpec((1, S, hw), lambda b, h, i: (b, 0, v_off + h)),
                  pl.BlockSpec((S, HEAD_DIM), lambda b, h, i: (0, 0)),
                  pl.BlockSpec((S, HEAD_DIM), lambda b, h, i: (0, 0)),
                  pl.BlockSpec((tq, HEAD_DIM), lambda b, h, i: (i, 0)),
                  pl.BlockSpec((tq, HEAD_DIM), lambda b, h, i: (i, 0)),
                  pl.BlockSpec((1, hw), lambda b, h, i: (0, 0))],
        out_specs=pl.BlockSpec((1, tq, hw), lambda b, h, i: (b, i, h)),
        out_shape=jax.ShapeDtypeStruct((B, S, n_heads * hw), bf16),
        scratch_shapes=[pltpu.VMEM((2, S, HEAD_DIM), bf16)],
        compiler_params=_params("parallel", "parallel", "arbitrary"),
        name="attn_b",
    )(lam_params, proj, proj, proj, cos, sin, cos, sin, g_sub.reshape(1, hw))


def _attn_a_kernel(q_ref, k_ref, v_ref, cos_ref, sin_ref, o_ref, qf, kf, vf, *scr, S, scale):
    nb = len(DIL_PAIRS)
    o_scr, l_scr = scr[:nb], scr[nb:]
    cos = cos_ref[...]
    sin = sin_ref[...]
    qf[...] = _rope(q_ref[0].astype(f32), cos, sin)
    kf[...] = _rope(k_ref[0].astype(f32), cos, sin)
    vf[...] = v_ref[0].astype(f32)

    for bi, (window, dil) in enumerate(DIL_PAIRS):
        L = S // dil
        half = window // (2 * dil)
        qc_n = min(128, L)
        kw_n = min(L, qc_n + 2 * half)
        n_chunks = L // qc_n

        def rows(start, n, dil=dil):
            return pl.ds(start, n) if dil == 1 else pl.ds(start, n, stride=dil)

        def body(idx, carry, dil=dil, half=half, qc_n=qc_n, kw_n=kw_n, n_chunks=n_chunks, L=L,
                 bi=bi, rows=rows):
            r = idx // n_chunks
            q0 = (idx % n_chunks) * qc_n
            k0 = jnp.clip(q0 - half, 0, L - kw_n)
            q_rows = rows(r + dil * q0, qc_n)
            k_rows = rows(r + dil * k0, kw_n)
            qc = qf[q_rows, :].astype(bf16)
            kc = kf[k_rows, :].astype(bf16)
            vc = vf[k_rows, :].astype(bf16)
            s = lax.dot_general(qc, kc, _NT, preferred_element_type=f32) * scale
            qpos = q0 + lax.broadcasted_iota(jnp.int32, (qc_n, kw_n), 0)
            kpos = k0 + lax.broadcasted_iota(jnp.int32, (qc_n, kw_n), 1)
            s = jnp.where(jnp.abs(kpos - qpos) <= half, s, NEG)
            m = jnp.max(s, axis=-1, keepdims=True)
            p = jnp.exp(s - m)
            den = jnp.sum(p, axis=-1, keepdims=True)
            o_scr[bi][q_rows, :] = jnp.dot((p / den).astype(bf16), vc, preferred_element_type=f32)
            l_scr[bi][q_rows, :] = jnp.broadcast_to(m + jnp.log(den), (qc_n, HEAD_DIM))
            return carry

        n_steps = dil * n_chunks
        lax.fori_loop(0, n_steps, body, 0, unroll=math.gcd(n_steps, CHUNK_UNROLL))

    lses = [l_scr[bi][...] for bi in range(nb)]
    m = functools.reduce(jnp.maximum, lses)
    es = [jnp.exp(l - m) for l in lses]
    z = functools.reduce(lambda a, b: a + b, es)
    out = functools.reduce(lambda a, b: a + b, [(e / z) * o_scr[bi][...] for bi, e in enumerate(es)])
    o_ref[0] = out.astype(o_ref.dtype)


def _attn_a(proj, cos, sin, n_heads):
    B, S, _ = proj.shape
    nb = len(DIL_PAIRS)
    blk = lambda off: pl.BlockSpec((1, S, HEAD_DIM), lambda b, h: (b, 0, off + h))
    tab = pl.BlockSpec((S, HEAD_DIM), lambda b, h: (0, 0))
    return pl.pallas_call(
        functools.partial(_attn_a_kernel, S=S, scale=HEAD_DIM ** -0.5),
        grid=(B, n_heads),
        in_specs=[blk(0), blk(n_heads), blk(2 * n_heads), tab, tab],
        out_specs=pl.BlockSpec((1, S, HEAD_DIM), lambda b, h: (b, 0, h)),
        out_shape=jax.ShapeDtypeStruct((B, S, n_heads * HEAD_DIM), bf16),
        scratch_shapes=[pltpu.VMEM((S, HEAD_DIM), f32), pltpu.VMEM((S, HEAD_DIM), f32),
                        pltpu.VMEM((S, HEAD_DIM), f32)]
                       + [pltpu.VMEM((S, HEAD_DIM), f32) for _ in range(2 * nb)],
        compiler_params=_params("parallel", "parallel"),
        name="attn_a",
    )(proj, proj, proj, cos, sin)


def _take_topk(s, order, k, break_ties):
    last = jnp.iinfo(jnp.int32).max
    taken = jnp.full(s.shape, k, jnp.int32)
    vals = []
    for i in range(k):
        m = jnp.max(s, axis=0, keepdims=True)
        hit = s == m
        if break_ties:
            hit = order == jnp.min(jnp.where(hit, order, last), axis=0, keepdims=True)
        s = jnp.where(hit, -jnp.inf, s)
        taken = jnp.where(hit, i, taken)
        vals.append(m)
    return jnp.concatenate(vals, axis=0), taken


def _peer_tables(scores, break_ties):
    K = PEER_TOPK
    half = K // 2
    n = scores[0].shape[1]
    tops, ranks = [], []
    for s in scores:
        top, rank = _take_topk(s, lax.broadcasted_iota(jnp.int32, s.shape, 0), K, break_ties)
        tops.append(top)
        ranks.append(rank)
    v1, v2 = tops
    i16 = lax.broadcasted_iota(jnp.int32, (K, n), 0)
    i8 = lax.broadcasted_iota(jnp.int32, (half, n), 0)
    cand = [v1 + v2[0:1]]
    order = [i16 * K]
    for j in range(1, half):
        cand.append(v1[0:half] + v2[j:j + 1])
        order.append(i8 * K + j)
    cand.append(v1[0:1] + v2[half:K])
    order.append(i8 + half)
    top, taken = _take_topk(jnp.concatenate(cand, axis=0), jnp.concatenate(order, axis=0), K, break_ties)
    kept = (taken < K).astype(f32)
    upper = kept[0:half]
    for j in range(1, half):
        upper = upper + kept[K + (j - 1) * half:K + j * half]
    row0 = jnp.sum(kept[K + (half - 1) * half:], axis=0, keepdims=True)
    upper = upper + jnp.where(i8 == 0, row0, 0.0)
    width = jnp.concatenate([upper, kept[half:K]], axis=0)
    width_of_key = jnp.zeros(scores[0].shape, f32)
    for i in range(K):
        width_of_key = jnp.where(ranks[0] == i, width[i:i + 1], width_of_key)
    z = jnp.sum(jnp.exp(top - top[0:1]), axis=0, keepdims=True)
    tables = (width_of_key, jnp.exp(scores[0] - v1[0:1]) / z,
              ranks[1].astype(f32), jnp.exp(scores[1] - v2[0:1]))
    n_taken = [jnp.sum((r < K).astype(f32), axis=0, keepdims=True) for r in ranks]
    n_taken.append(jnp.sum(kept, axis=0, keepdims=True))
    over = jnp.max(functools.reduce(jnp.maximum, n_taken)) > K
    return tables, over


def _peer_retrieve_kernel(q_ref, keys_ref, p1_ref, p2_ref):
    scores = []
    for c in range(2):
        qc = q_ref[:, c * N_KEYS:(c + 1) * N_KEYS].astype(bf16)
        kc = keys_ref[0, c].astype(bf16)
        scores.append(lax.dot_general(kc, qc, _NT, preferred_element_type=f32))

    def write(tables):
        p1_ref[0, 0], p1_ref[0, 1], p2_ref[0, 0], p2_ref[0, 1] = tables

    tables, tied = _peer_tables(scores, break_ties=False)
    write(tables)

    @pl.when(tied)
    def _():
        write(_peer_tables(scores, break_ties=True)[0])


def _peer_retrieve(q, sub_keys):
    T = q.shape[0]
    tm = _largest_divisor(T, (256, 128))
    p_spec = pl.BlockSpec((1, 2, N_KEYS, tm), lambda i, h: (h, 0, 0, i))
    p_shape = jax.ShapeDtypeStruct((PEER_HEADS, 2, N_KEYS, T), f32)
    return pl.pallas_call(
        _peer_retrieve_kernel,
        grid=(T // tm, PEER_HEADS),
        in_specs=[pl.BlockSpec((tm, 2 * N_KEYS), lambda i, h: (i, h)),
                  pl.BlockSpec((1, 2, N_KEYS, HEAD_DIM), lambda i, h: (h, 0, 0, 0))],
        out_specs=[p_spec, p_spec],
        out_shape=[p_shape, p_shape],
        compiler_params=_params("parallel", "parallel"),
        name="peer_retrieve",
    )(q, sub_keys)


PEER_EB = 1024
GATE_ROWS = 16


def _gelu(x):
    return 0.5 * x * (1.0 + lax.erf(x * (1.0 / math.sqrt(2.0))))


def _peer_expert_kernel(x_ref, hb_ref, p1_ref, p2_ref, u_ref, vt_ref, o_ref, acc_ref, at_ref):
    j = pl.program_id(1)
    eb, tm = at_ref.shape
    heads = range(PEER_HEADS)
    group = 2 * N_KEYS

    @pl.when(j == 0)
    def _():
        acc_ref[...] = jnp.zeros_like(acc_ref)

    def body(g, carry):
        u_rows = u_ref[pl.ds(pl.multiple_of(g * group, group), group), :]
        w1_rows = [[p1_ref[h, 0, pl.ds(g * (group // N_KEYS) + aa, 1), :] for h in heads]
                   for aa in range(group // N_KEYS)]
        e1_rows = [[p1_ref[h, 1, pl.ds(g * (group // N_KEYS) + aa, 1), :] for h in heads]
                   for aa in range(group // N_KEYS)]
        for t0 in range(0, tm, MXU_N):
            ht = lax.dot_general(u_rows, hb_ref[t0:t0 + MXU_N, :], _NT, preferred_element_type=f32)
            for aa in range(group // N_KEYS):
                sb = g * (group // N_KEYS) + aa
                for l0 in range(t0, t0 + MXU_N, LANES):
                    lanes = slice(l0, l0 + LANES)
                    w1 = [row[:, lanes] for row in w1_rows[aa]]
                    e1 = [row[:, lanes] for row in e1_rows[aa]]
                    for r in range(0, N_KEYS, GATE_ROWS):
                        krows = slice(r, r + GATE_ROWS)
                        gate = None
                        for h in heads:
                            hit = p2_ref[h, 0, krows, lanes] < w1[h]
                            w = jnp.where(hit, e1[h] * p2_ref[h, 1, krows, lanes], 0.0)
                            gate = w if gate is None else gate + w
                        hrows = slice(aa * N_KEYS + r, aa * N_KEYS + r + GATE_ROWS)
                        erows = pl.ds(pl.multiple_of(sb * N_KEYS + r, GATE_ROWS), GATE_ROWS)
                        act = gate * _gelu(ht[hrows, l0 - t0:l0 - t0 + LANES])
                        at_ref[erows, lanes] = act.astype(bf16)
        return carry

    lax.fori_loop(0, eb // group, body, 0)
    acc_ref[...] += jnp.dot(vt_ref[...], at_ref[...], preferred_element_type=f32)

    @pl.when(j == pl.num_programs(1) - 1)
    def _():
        o_ref[...] = x_ref[...] + acc_ref[...].T


def _peer_expert(x, hb, p1, p2, u, vt):
    T, D = x.shape
    E = u.shape[0]
    tm = _largest_divisor(T, (512, 256, 128))
    eb = PEER_EB
    once = dict(pipeline_mode=pl.Buffered(1))
    return pl.pallas_call(
        _peer_expert_kernel,
        grid=(T // tm, E // eb),
        in_specs=[pl.BlockSpec((tm, D), lambda i, j: (i, 0), **once),
                  pl.BlockSpec((tm, D), lambda i, j: (i, 0), **once),
                  pl.BlockSpec((PEER_HEADS, 2, eb // N_KEYS, tm), lambda i, j: (0, 0, j, i)),
                  pl.BlockSpec((PEER_HEADS, 2, N_KEYS, tm), lambda i, j: (0, 0, 0, i), **once),
                  pl.BlockSpec((eb, D), lambda i, j: (j, 0)),
                  pl.BlockSpec((D, eb), lambda i, j: (0, j))],
        out_specs=pl.BlockSpec((tm, D), lambda i, j: (i, 0)),
        out_shape=jax.ShapeDtypeStruct((T, D), f32),
        scratch_shapes=[pltpu.VMEM((D, tm), f32), pltpu.VMEM((eb, tm), bf16)],
        compiler_params=_params("parallel", "arbitrary"),
        name="peer_expert",
    )(x, hb, p1, p2, u, vt)


def _final_norm_kernel(x_ref, g_ref, o_ref):
    o_ref[...] = _rms(x_ref[...], g_ref[...])


def _final_norm(x, g):
    T, D = x.shape
    tm = _largest_divisor(T, (512, 256, 128))
    return pl.pallas_call(
        _final_norm_kernel,
        grid=(T // tm,),
        in_specs=[pl.BlockSpec((tm, D), lambda i: (i, 0)), pl.BlockSpec((1, D), lambda i: (0, 0))],
        out_specs=pl.BlockSpec((tm, D), lambda i: (i, 0)),
        out_shape=jax.ShapeDtypeStruct((T, D), f32),
        compiler_params=_params("parallel"),
        name="final_norm",
    )(x, g.reshape(1, D))


def _trunk(x, w):
    B, S, D = x.shape
    T = B * S
    depth = w["g_mix"].shape[0]
    heads_a = D // (2 * HEAD_DIM)
    heads_b = D // (4 * HEAD_DIM)
    heads_c = D // HEAD_DIM
    kv_c = heads_c // 4
    cos, sin = _rope_tables_full(S)
    cos_ax, sin_ax = _rope_tables_axial(S)
    xf = x.reshape(T, D)
    for l in range(depth):
        if l % 2 == 0:
            e = l // 2
            lam_init = 0.8 - 0.6 * math.exp(-0.3 * l)
            proj = _norm_proj(xf, w["g_mix"][l], w["w_in_ab"][e], bf16).reshape(B, S, -1)
            oa = _attn_a(proj, cos, sin, heads_a)
            q_off = 3 * heads_a // 2
            ob = _attn_b(proj, w["lam"][e], w["g_subln"][e], cos, sin, lam_init,
                         q_off, q_off + heads_b, q_off + 2 * heads_b, heads_b)
            xf = _out_proj(xf, [oa.reshape(T, -1), ob.reshape(T, -1)], w["w_out_ab"][e])
        else:
            o = l // 2
            proj = _norm_proj(xf, w["g_mix"][l], w["w_qkv_c"][o], bf16).reshape(B, S, -1)
            oc = _attn_c(proj, w["g_qnorm"][o], w["g_knorm"][o], cos_ax, sin_ax, heads_c, kv_c)
            xf = _out_proj(xf, [oc.reshape(T, -1)], w["w_out_c"][o])
        q, hb = _norm_proj(xf, w["g_ffn"][l], w["w_peer_q"][l], f32, emit_h=True)
        p1, p2 = _peer_retrieve(q, w["peer_sub_keys"][l])
        xf = _peer_expert(xf, hb, p1, p2, w["peer_u"][l], w["peer_vt"][l])
    return _final_norm(xf, w["g_final"]).reshape(B, S, D)


def kernel(x_prompt, x_sample, g_mix, g_ffn, g_final, w_in_ab, w_out_ab, lam_q1, lam_k1, lam_q2, lam_k2,
           g_subln, w_qkv_c, w_out_c, g_qnorm, g_knorm, w_peer_q, peer_sub_keys, peer_u, peer_v):
    w = dict(
        g_mix=g_mix, g_ffn=g_ffn, g_final=g_final,
        w_in_ab=w_in_ab.astype(bf16), w_out_ab=w_out_ab.astype(bf16),
        lam=jnp.stack([lam_q1, lam_k1, lam_q2, lam_k2], axis=1).astype(f32),
        g_subln=g_subln, w_qkv_c=w_qkv_c.astype(bf16), w_out_c=w_out_c.astype(bf16),
        g_qnorm=g_qnorm, g_knorm=g_knorm, w_peer_q=w_peer_q.astype(bf16),
        peer_sub_keys=peer_sub_keys, peer_u=peer_u.astype(bf16),
        peer_vt=jnp.swapaxes(peer_v.astype(bf16), 1, 2),
    )
    return (_trunk(x_prompt, w), _trunk(x_sample, w))
```

```python
import functools
import math

import jax
import jax.numpy as jnp
from jax import lax
from jax.experimental import pallas as pl
from jax.experimental.pallas import tpu as pltpu

f32 = jnp.float32
bf16 = jnp.bfloat16

HEAD_DIM = 128
LANES = 128
SUBLANES = 8
MXU_N = 256
ROPE_THETA = 10000.0
GRID_W = 64
EPS = 1e-6
NEG = -1e30
DIL_PAIRS = ((128, 1), (512, 4), (2048, 16))
CHUNK_UNROLL = 8
PEER_HEADS = 8
N_KEYS = 128
PEER_TOPK = 16
VMEM_LIMIT = 60 * 1024 * 1024

_NT = (((1,), (1,)), ((), ()))


def _largest_divisor(n, candidates):
    for c in candidates:
        if n % c == 0:
            return c
    return n


def _params(*sem, flags=None):
    return pltpu.CompilerParams(dimension_semantics=sem, vmem_limit_bytes=VMEM_LIMIT, flags=flags)


def _rms(x, g):
    return x * lax.rsqrt(jnp.mean(x * x, axis=-1, keepdims=True) + EPS) * g


def _norm_proj_kernel(x_ref, g_ref, w_ref, *rest, emit_h):
    if emit_h:
        o_ref, hb_ref, h_scr = rest
    else:
        o_ref, h_scr = rest

    @pl.when(pl.program_id(1) == 0)
    def _():
        hb = _rms(x_ref[...], g_ref[...]).astype(bf16)
        h_scr[...] = hb
        if emit_h:
            hb_ref[...] = hb

    o_ref[...] = jnp.dot(h_scr[...], w_ref[...], preferred_element_type=f32).astype(o_ref.dtype)


def _norm_proj(x, g, w, out_dtype, emit_h=False):
    T, D = x.shape
    N = w.shape[1]
    tm = _largest_divisor(T, (1024, 512, 256, 128))
    tn = _largest_divisor(N, (1024, 512, 256, 128))
    out_shape = [jax.ShapeDtypeStruct((T, N), out_dtype)]
    out_specs = [pl.BlockSpec((tm, tn), lambda i, j: (i, j))]
    if emit_h:
        out_shape.append(jax.ShapeDtypeStruct((T, D), bf16))
        out_specs.append(pl.BlockSpec((tm, D), lambda i, j: (i, 0)))
    res = pl.pallas_call(
        functools.partial(_norm_proj_kernel, emit_h=emit_h),
        grid=(T // tm, N // tn),
        in_specs=[pl.BlockSpec((tm, D), lambda i, j: (i, 0)),
                  pl.BlockSpec((1, D), lambda i, j: (0, 0)),
                  pl.BlockSpec((D, tn), lambda i, j: (0, j))],
        out_specs=out_specs,
        out_shape=out_shape,
        scratch_shapes=[pltpu.VMEM((tm, D), bf16)],
        compiler_params=_params("parallel", "arbitrary"),
        name="norm_proj",
    )(x, g.reshape(1, D), w)
    return res if emit_h else res[0]


def _out_proj_kernel(*refs, n_in):
    res_ref = refs[0]
    a_refs = refs[1:1 + n_in]
    w_refs = refs[1 + n_in:1 + 2 * n_in]
    o_ref = refs[1 + 2 * n_in]
    acc = res_ref[...]
    for a_ref, w_ref in zip(a_refs, w_refs):
        acc = acc + jnp.dot(a_ref[...], w_ref[...], preferred_element_type=f32)
    o_ref[...] = acc


def _out_proj(res, acts, w):
    T, N = res.shape
    tm = _largest_divisor(T, (1024, 512, 256, 128))
    tn = _largest_divisor(N, (1024, 512, 256, 128))
    n_in = len(acts)
    K = acts[0].shape[1]
    assert all(a.shape[1] == K for a in acts) and w.shape[0] == n_in * K
    in_specs = [pl.BlockSpec((tm, tn), lambda i, j: (i, j))]
    in_specs += [pl.BlockSpec((tm, K), lambda i, j: (i, 0)) for _ in acts]
    in_specs += [pl.BlockSpec((K, tn), functools.partial(lambda i, j, kk: (kk, j), kk=kk))
                 for kk in range(n_in)]
    return pl.pallas_call(
        functools.partial(_out_proj_kernel, n_in=n_in),
        grid=(T // tm, N // tn),
        in_specs=in_specs,
        out_specs=pl.BlockSpec((tm, tn), lambda i, j: (i, j)),
        out_shape=jax.ShapeDtypeStruct((T, N), f32),
        compiler_params=_params("parallel", "parallel"),
        name="out_proj",
    )(res, *acts, *([w] * n_in))


def _rope_tables(pos, dim):
    inv = ROPE_THETA ** (-jnp.arange(0, dim, 2, dtype=f32) / dim)
    ang = pos[:, None] * inv[None, :]
    ang = jnp.concatenate([ang, ang], axis=-1)
    return jnp.cos(ang), jnp.sin(ang)


def _rope_tables_full(S):
    cos, sin = _rope_tables(jnp.arange(S, dtype=f32), HEAD_DIM)
    half = HEAD_DIM // 2
    sign = jnp.where(jnp.arange(HEAD_DIM) < half, -1.0, 1.0).astype(f32)
    return cos, sin * sign[None, :]


def _rope_tables_axial(S):
    rows = S // GRID_W
    row_idx = jnp.repeat(jnp.arange(rows, dtype=f32), GRID_W)
    col_idx = jnp.tile(jnp.arange(GRID_W, dtype=f32), rows)
    cr, sr = _rope_tables(row_idx, HEAD_DIM // 2)
    cc, sc = _rope_tables(col_idx, HEAD_DIM // 2)
    quarter = HEAD_DIM // 4
    sign = jnp.where(jnp.arange(HEAD_DIM // 2) < quarter, -1.0, 1.0).astype(f32)
    cos = jnp.concatenate([cr, cc], axis=-1)
    sin = jnp.concatenate([sr * sign[None, :], sc * sign[None, :]], axis=-1)
    return cos, sin


def _rope(x, cos, sin_signed):
    return x * cos + pltpu.roll(x, HEAD_DIM // 2, axis=1) * sin_signed


def _rope_axial(x, cos, sin_signed):
    quarter = HEAD_DIM // 4
    lane = lax.broadcasted_iota(jnp.int32, x.shape, 1)
    lower = (lane % (2 * quarter)) < quarter
    rot = jnp.where(lower, pltpu.roll(x, HEAD_DIM - quarter, axis=1), pltpu.roll(x, quarter, axis=1))
    return x * cos + rot * sin_signed


def _softmax_pv(s, v):
    m = jnp.max(s, axis=-1, keepdims=True)
    p = jnp.exp(s - m)
    l = jnp.sum(p, axis=-1, keepdims=True)
    return jnp.dot(p.astype(bf16), v, preferred_element_type=f32) / l


def _softmax_pv_ones(s, v_ones):
    d = v_ones.shape[1] // 2
    p = jnp.exp(s - jnp.max(s, axis=-1, keepdims=True))
    pv = jnp.dot(p.astype(bf16), v_ones, preferred_element_type=f32)
    return pv[:, :d] / pv[:, d:]


def _attn_c_kernel(q_ref, k_ref, v_ref, cos_ref, sin_ref, cosq_ref, sinq_ref, gq_ref, gk_ref,
                   o_ref, k_scr, v_scr, *, groups, scale):
    @pl.when(pl.program_id(2) == 0)
    def _():
        k = _rms(k_ref[0].astype(f32), gk_ref[...])
        k_scr[...] = _rope_axial(k, cos_ref[...], sin_ref[...]).astype(bf16)
        v_scr[:, :HEAD_DIM] = v_ref[0]
        v_scr[:, HEAD_DIM:] = jnp.ones_like(v_ref[0])

    cq = cosq_ref[...]
    sq = sinq_ref[...]
    gq = gq_ref[...]
    for g in range(groups):
        cols = slice(g * HEAD_DIM, (g + 1) * HEAD_DIM)
        q = _rms(q_ref[0, :, cols].astype(f32), gq)
        q = (_rope_axial(q, cq, sq) * scale).astype(bf16)
        s = lax.dot_general(q, k_scr[...], _NT, preferred_element_type=f32)
        o_ref[0, :, cols] = _softmax_pv_ones(s, v_scr[...]).astype(o_ref.dtype)


def _attn_c(proj, g_q, g_k, cos, sin, n_heads, n_kv):
    B, S, _ = proj.shape
    groups = n_heads // n_kv
    tq = _largest_divisor(S, (256, 128))
    qw = groups * HEAD_DIM
    k_off = n_heads
    v_off = n_heads + n_kv
    return pl.pallas_call(
        functools.partial(_attn_c_kernel, groups=groups, scale=HEAD_DIM ** -0.5),
        grid=(B, n_kv, S // tq),
        in_specs=[pl.BlockSpec((1, tq, qw), lambda b, h, i: (b, i, h)),
                  pl.BlockSpec((1, S, HEAD_DIM), lambda b, h, i: (b, 0, k_off + h)),
                  pl.BlockSpec((1, S, HEAD_DIM), lambda b, h, i: (b, 0, v_off + h)),
                  pl.BlockSpec((S, HEAD_DIM), lambda b, h, i: (0, 0)),
                  pl.BlockSpec((S, HEAD_DIM), lambda b, h, i: (0, 0)),
                  pl.BlockSpec((tq, HEAD_DIM), lambda b, h, i: (i, 0)),
                  pl.BlockSpec((tq, HEAD_DIM), lambda b, h, i: (i, 0)),
                  pl.BlockSpec((1, HEAD_DIM), lambda b, h, i: (0, 0)),
                  pl.BlockSpec((1, HEAD_DIM), lambda b, h, i: (0, 0))],
        out_specs=pl.BlockSpec((1, tq, qw), lambda b, h, i: (b, i, h)),
        out_shape=jax.ShapeDtypeStruct((B, S, n_heads * HEAD_DIM), bf16),
        scratch_shapes=[pltpu.VMEM((S, HEAD_DIM), bf16), pltpu.VMEM((S, 2 * HEAD_DIM), bf16)],
        compiler_params=_params("parallel", "parallel", "arbitrary"),
        name="attn_c",
    )(proj, proj, proj, cos, sin, cos, sin, g_q.reshape(1, HEAD_DIM), g_k.reshape(1, HEAD_DIM))


def _attn_b_kernel(lam_ref, q_ref, k_ref, v_ref, cos_ref, sin_ref, cosq_ref, sinq_ref, gsub_ref,
                   o_ref, k_scr, *, scale, lam_init):
    @pl.when(pl.program_id(2) == 0)
    def _():
        for c in range(2):
            k = k_ref[0, :, c * HEAD_DIM:(c + 1) * HEAD_DIM].astype(f32)
            k_scr[c] = _rope(k, cos_ref[...], sin_ref[...]).astype(bf16)

    lam_p = lam_ref[...]
    lam = (jnp.exp(jnp.sum(lam_p[0:1] * lam_p[1:2], axis=-1, keepdims=True))
           - jnp.exp(jnp.sum(lam_p[2:3] * lam_p[3:4], axis=-1, keepdims=True)) + lam_init)
    cq = cosq_ref[...]
    sq = sinq_ref[...]
    v = v_ref[0]
    outs = []
    for c in range(2):
        q = q_ref[0, :, c * HEAD_DIM:(c + 1) * HEAD_DIM].astype(f32)
        q = (_rope(q, cq, sq) * scale).astype(bf16)
        s = lax.dot_general(q, k_scr[c], _NT, preferred_element_type=f32)
        outs.append(_softmax_pv(s, v))
    o = outs[0] - lam * outs[1]
    o = _rms(o, gsub_ref[...]) * (1.0 - lam_init)
    o_ref[0] = o.astype(o_ref.dtype)


def _attn_b(proj, lam_params, g_sub, cos, sin, lam_init, q_off, k_off, v_off, n_heads):
    B, S, _ = proj.shape
    tq = _largest_divisor(S, (256, 128))
    hw = 2 * HEAD_DIM
    return pl.pallas_call(
        functools.partial(_attn_b_kernel, scale=HEAD_DIM ** -0.5, lam_init=lam_init),
        grid=(B, n_heads, S // tq),
        in_specs=[pl.BlockSpec((4, HEAD_DIM), lambda b, h, i: (0, 0)),
                  pl.BlockSpec((1, tq, hw), lambda b, h, i: (b, i, q_off + h)),
                  pl.BlockSpec((1, S, hw), lambda b, h, i: (b, 0, k_off + h)),
                  pl.BlockSpec((1, S, hw), lambda b, h, i: (b, 0, v_off + h)),
                  pl.BlockSpec((S, HEAD_DIM), lambda b, h, i: (0, 0)),
                  pl.BlockSpec((S, HEAD_DIM), lambda b, h, i: (0, 0)),
                  pl.BlockSpec((tq, HEAD_DIM), lambda b, h, i: (i, 0)),
                  pl.BlockSpec((tq, HEAD_DIM), lambda b, h, i: (i, 0)),
                  pl.BlockSpec((1, hw), lambda b, h, i: (0, 0))],
        out_specs=pl.BlockSpec((1, tq, hw), lambda b, h, i: (b, i, h)),
        out_shape=jax.ShapeDtypeStruct((B, S, n_heads * hw), bf16),
        scratch_shapes=[pltpu.VMEM((2, S, HEAD_DIM), bf16)],
        compiler_params=_params("parallel", "parallel", "arbitrary"),
        name="attn_b",
    )(lam_params, proj, proj, proj, cos, sin, cos, sin, g_sub.reshape(1, hw))


def _attn_a_kernel(q_ref, k_ref, v_ref, cos_ref, sin_ref, o_ref, qf, kf, vf, *scr, S, scale):
    nb = len(DIL_PAIRS)
    o_scr, l_scr = scr[:nb], scr[nb:]
    cos = cos_ref[...]
    sin = sin_ref[...]
    qf[...] = _rope(q_ref[0].astype(f32), cos, sin) * scale
    kf[...] = _rope(k_ref[0].astype(f32), cos, sin)
    vf[...] = v_ref[0].astype(f32)

    for bi, (window, dil) in enumerate(DIL_PAIRS):
        L = S // dil
        half = window // (2 * dil)
        qc_n = min(128, L)
        kw_n = min(L, qc_n + 2 * half)
        n_chunks = L // qc_n

        def rows(start, n, dil=dil):
            return pl.ds(start, n) if dil == 1 else pl.ds(start, n, stride=dil)

        def body(idx, carry, dil=dil, half=half, qc_n=qc_n, kw_n=kw_n, n_chunks=n_chunks, L=L,
                 bi=bi, rows=rows):
            r = idx // n_chunks
            q0 = (idx % n_chunks) * qc_n
            k0 = jnp.clip(q0 - half, 0, L - kw_n)
            q_rows = rows(r + dil * q0, qc_n)
            k_rows = rows(r + dil * k0, kw_n)
            qc = qf[q_rows, :].astype(bf16)
            kc = kf[k_rows, :].astype(bf16)
            vc = vf[k_rows, :].astype(bf16)
            s = lax.dot_general(qc, kc, _NT, preferred_element_type=f32)
            qpos = q0 + lax.broadcasted_iota(jnp.int32, (qc_n, kw_n), 0)
            kpos = k0 + lax.broadcasted_iota(jnp.int32, (qc_n, kw_n), 1)
            s = jnp.where(jnp.abs(kpos - qpos) <= half, s, NEG)
            m = jnp.max(s, axis=-1, keepdims=True)
            p = jnp.exp(s - m)
            den = jnp.sum(p, axis=-1, keepdims=True)
            o_scr[bi][q_rows, :] = jnp.dot((p / den).astype(bf16), vc, preferred_element_type=f32)
            l_scr[bi][q_rows, :] = jnp.broadcast_to(m + jnp.log(den), (qc_n, HEAD_DIM))
            return carry

        n_steps = dil * n_chunks
        lax.fori_loop(0, n_steps, body, 0, unroll=math.gcd(n_steps, CHUNK_UNROLL))

    lses = [l_scr[bi][...] for bi in range(nb)]
    m = functools.reduce(jnp.maximum, lses)
    es = [jnp.exp(l - m) for l in lses]
    z = functools.reduce(lambda a, b: a + b, es)
    out = functools.reduce(lambda a, b: a + b, [(e / z) * o_scr[bi][...] for bi, e in enumerate(es)])
    o_ref[0] = out.astype(o_ref.dtype)


def _attn_a(proj, cos, sin, n_heads):
    B, S, _ = proj.shape
    nb = len(DIL_PAIRS)
    blk = lambda off: pl.BlockSpec((1, S, HEAD_DIM), lambda b, h: (b, 0, off + h))
    tab = pl.BlockSpec((S, HEAD_DIM), lambda b, h: (0, 0))
    return pl.pallas_call(
        functools.partial(_attn_a_kernel, S=S, scale=HEAD_DIM ** -0.5),
        grid=(B, n_heads),
        in_specs=[blk(0), blk(n_heads), blk(2 * n_heads), tab, tab],
        out_specs=pl.BlockSpec((1, S, HEAD_DIM), lambda b, h: (b, 0, h)),
        out_shape=jax.ShapeDtypeStruct((B, S, n_heads * HEAD_DIM), bf16),
        scratch_shapes=[pltpu.VMEM((S, HEAD_DIM), f32), pltpu.VMEM((S, HEAD_DIM), f32),
                        pltpu.VMEM((S, HEAD_DIM), f32)]
                       + [pltpu.VMEM((S, HEAD_DIM), f32) for _ in range(2 * nb)],
        compiler_params=_params("parallel", "parallel"),
        name="attn_a",
    )(proj, proj, proj, cos, sin)


def _take_topk(s, order, k, break_ties):
    last = jnp.iinfo(jnp.int32).max
    taken = jnp.full(s.shape, k, jnp.int32)
    vals = []
    for i in range(k):
        m = jnp.max(s, axis=0, keepdims=True)
        hit = s == m
        if break_ties:
            hit = order == jnp.min(jnp.where(hit, order, last), axis=0, keepdims=True)
        s = jnp.where(hit, -jnp.inf, s)
        taken = jnp.where(hit, i, taken)
        vals.append(m)
    return jnp.concatenate(vals, axis=0), taken


def _peer_tables(scores, break_ties):
    K = PEER_TOPK
    half = K // 2
    n = scores[0].shape[1]
    tops, ranks = [], []
    for s in scores:
        top, rank = _take_topk(s, lax.broadcasted_iota(jnp.int32, s.shape, 0), K, break_ties)
        tops.append(top)
        ranks.append(rank)
    v1, v2 = tops
    i16 = lax.broadcasted_iota(jnp.int32, (K, n), 0)
    i8 = lax.broadcasted_iota(jnp.int32, (half, n), 0)
    cand = [v1 + v2[0:1]]
    order = [i16 * K]
    for j in range(1, half):
        cand.append(v1[0:half] + v2[j:j + 1])
        order.append(i8 * K + j)
    cand.append(v1[0:1] + v2[half:K])
    order.append(i8 + half)
    top, taken = _take_topk(jnp.concatenate(cand, axis=0), jnp.concatenate(order, axis=0), K, break_ties)
    kept = (taken < K).astype(f32)
    upper = kept[0:half]
    for j in range(1, half):
        upper = upper + kept[K + (j - 1) * half:K + j * half]
    row0 = jnp.sum(kept[K + (half - 1) * half:], axis=0, keepdims=True)
    upper = upper + jnp.where(i8 == 0, row0, 0.0)
    width = jnp.concatenate([upper, kept[half:K]], axis=0)
    width_of_key = jnp.zeros(scores[0].shape, f32)
    for i in range(K):
        width_of_key = jnp.where(ranks[0] == i, width[i:i + 1], width_of_key)
    z = jnp.sum(jnp.exp(top - top[0:1]), axis=0, keepdims=True)
    tables = (width_of_key, jnp.exp(scores[0] - v1[0:1]) / z,
              ranks[1].astype(f32), jnp.exp(scores[1] - v2[0:1]))
    n_taken = [jnp.sum((r < K).astype(f32), axis=0, keepdims=True) for r in ranks]
    n_taken.append(jnp.sum(kept, axis=0, keepdims=True))
    over = jnp.max(functools.reduce(jnp.maximum, n_taken)) > K
    return tables, over


def _peer_retrieve_kernel(q_ref, keys_ref, p1_ref, p2_ref):
    scores = []
    for c in range(2):
        qc = q_ref[:, c * N_KEYS:(c + 1) * N_KEYS].astype(bf16)
        kc = keys_ref[0, c].astype(bf16)
        scores.append(lax.dot_general(kc, qc, _NT, preferred_element_type=f32))

    def write(tables):
        p1_ref[0, 0], p1_ref[0, 1], p2_ref[0, 0], p2_ref[0, 1] = tables

    tables, tied = _peer_tables(scores, break_ties=False)
    write(tables)

    @pl.when(tied)
    def _():
        write(_peer_tables(scores, break_ties=True)[0])


def _peer_retrieve(q, sub_keys):
    T = q.shape[0]
    tm = _largest_divisor(T, (256, 128))
    p_spec = pl.BlockSpec((1, 2, N_KEYS, tm), lambda i, h: (h, 0, 0, i))
    p_shape = jax.ShapeDtypeStruct((PEER_HEADS, 2, N_KEYS, T), f32)
    return pl.pallas_call(
        _peer_retrieve_kernel,
        grid=(T // tm, PEER_HEADS),
        in_specs=[pl.BlockSpec((tm, 2 * N_KEYS), lambda i, h: (i, h)),
                  pl.BlockSpec((1, 2, N_KEYS, HEAD_DIM), lambda i, h: (h, 0, 0, 0))],
        out_specs=[p_spec, p_spec],
        out_shape=[p_shape, p_shape],
        compiler_params=_params("parallel", "parallel"),
        name="peer_retrieve",
    )(q, sub_keys)


PEER_EB = 1024
GATE_ROWS = 16


def _gelu(x):
    return 0.5 * x * (1.0 + lax.erf(x * (1.0 / math.sqrt(2.0))))


def _peer_expert_kernel(x_ref, hb_ref, p1_ref, p2_ref, u_ref, vt_ref, o_ref, acc_ref, at_ref):
    j = pl.program_id(1)
    eb, tm = at_ref.shape
    heads = range(PEER_HEADS)
    group = 2 * N_KEYS

    @pl.when(j == 0)
    def _():
        acc_ref[...] = jnp.zeros_like(acc_ref)

    def body(g, carry):
        u_rows = u_ref[pl.ds(pl.multiple_of(g * group, group), group), :]
        w1_rows = [[p1_ref[h, 0, pl.ds(g * (group // N_KEYS) + aa, 1), :] for h in heads]
                   for aa in range(group // N_KEYS)]
        e1_rows = [[p1_ref[h, 1, pl.ds(g * (group // N_KEYS) + aa, 1), :] for h in heads]
                   for aa in range(group // N_KEYS)]
        for t0 in range(0, tm, MXU_N):
            ht = lax.dot_general(u_rows, hb_ref[t0:t0 + MXU_N, :], _NT, preferred_element_type=f32)
            for aa in range(group // N_KEYS):
                sb = g * (group // N_KEYS) + aa
                for l0 in range(t0, t0 + MXU_N, LANES):
                    lanes = slice(l0, l0 + LANES)
                    w1 = [row[:, lanes] for row in w1_rows[aa]]
                    e1 = [row[:, lanes] for row in e1_rows[aa]]
                    for r in range(0, N_KEYS, GATE_ROWS):
                        krows = slice(r, r + GATE_ROWS)
                        gate = None
                        for h in heads:
                            hit = p2_ref[h, 0, krows, lanes] < w1[h]
                            w = jnp.where(hit, e1[h] * p2_ref[h, 1, krows, lanes], 0.0)
                            gate = w if gate is None else gate + w
                        hrows = slice(aa * N_KEYS + r, aa * N_KEYS + r + GATE_ROWS)
                        erows = pl.ds(pl.multiple_of(sb * N_KEYS + r, GATE_ROWS), GATE_ROWS)
                        act = gate * _gelu(ht[hrows, l0 - t0:l0 - t0 + LANES])
                        at_ref[erows, lanes] = act.astype(bf16)
        return carry

    lax.fori_loop(0, eb // group, body, 0)
    acc_ref[...] += jnp.dot(vt_ref[...], at_ref[...], preferred_element_type=f32)

    @pl.when(j == pl.num_programs(1) - 1)
    def _():
        o_ref[...] = x_ref[...] + acc_ref[...].T


def _peer_expert(x, hb, p1, p2, u, vt):
    T, D = x.shape
    E = u.shape[0]
    tm = _largest_divisor(T, (512, 256, 128))
    eb = PEER_EB
    return pl.pallas_call(
        _peer_expert_kernel,
        grid=(T // tm, E // eb),
        in_specs=[pl.BlockSpec((tm, D), lambda i, j: (i, 0)),
                  pl.BlockSpec((tm, D), lambda i, j: (i, 0)),
                  pl.BlockSpec((PEER_HEADS, 2, eb // N_KEYS, tm), lambda i, j: (0, 0, j, i)),
                  pl.BlockSpec((PEER_HEADS, 2, N_KEYS, tm), lambda i, j: (0, 0, 0, i)),
                  pl.BlockSpec((eb, D), lambda i, j: (j, 0)),
                  pl.BlockSpec((None, D, eb), lambda i, j: (j, 0, 0))],
        out_specs=pl.BlockSpec((tm, D), lambda i, j: (i, 0)),
        out_shape=jax.ShapeDtypeStruct((T, D), f32),
        scratch_shapes=[pltpu.VMEM((D, tm), f32), pltpu.VMEM((eb, tm), bf16)],
        compiler_params=_params("parallel", "arbitrary"),
        name="peer_expert",
    )(x, hb, p1, p2, u, vt)


def _final_norm_kernel(x_ref, g_ref, o_ref):
    o_ref[...] = _rms(x_ref[...], g_ref[...])


def _final_norm(x, g):
    T, D = x.shape
    tm = _largest_divisor(T, (512, 256, 128))
    return pl.pallas_call(
        _final_norm_kernel,
        grid=(T // tm,),
        in_specs=[pl.BlockSpec((tm, D), lambda i: (i, 0)), pl.BlockSpec((1, D), lambda i: (0, 0))],
        out_specs=pl.BlockSpec((tm, D), lambda i: (i, 0)),
        out_shape=jax.ShapeDtypeStruct((T, D), f32),
        compiler_params=_params("parallel"),
        name="final_norm",
    )(x, g.reshape(1, D))


def _trunk(x, w):
    B, S, D = x.shape
    T = B * S
    depth = w["g_mix"].shape[0]
    heads_a = D // (2 * HEAD_DIM)
    heads_b = D // (4 * HEAD_DIM)
    heads_c = D // HEAD_DIM
    kv_c = heads_c // 4
    cos, sin = _rope_tables_full(S)
    cos_ax, sin_ax = _rope_tables_axial(S)
    xf = x.reshape(T, D)
    for l in range(depth):
        if l % 2 == 0:
            e = l // 2
            lam_init = 0.8 - 0.6 * math.exp(-0.3 * l)
            proj = _norm_proj(xf, w["g_mix"][l], w["w_in_ab"][e], bf16).reshape(B, S, -1)
            oa = _attn_a(proj, cos, sin, heads_a)
            q_off = 3 * heads_a // 2
            ob = _attn_b(proj, w["lam"][e], w["g_subln"][e], cos, sin, lam_init,
                         q_off, q_off + heads_b, q_off + 2 * heads_b, heads_b)
            xf = _out_proj(xf, [oa.reshape(T, -1), ob.reshape(T, -1)], w["w_out_ab"][e])
        else:
            o = l // 2
            proj = _norm_proj(xf, w["g_mix"][l], w["w_qkv_c"][o], bf16).reshape(B, S, -1)
            oc = _attn_c(proj, w["g_qnorm"][o], w["g_knorm"][o], cos_ax, sin_ax, heads_c, kv_c)
            xf = _out_proj(xf, [oc.reshape(T, -1)], w["w_out_c"][o])
        q, hb = _norm_proj(xf, w["g_ffn"][l], w["w_peer_q"][l], f32, emit_h=True)
        p1, p2 = _peer_retrieve(q, w["peer_sub_keys"][l])
        xf = _peer_expert(xf, hb, p1, p2, w["peer_u"][l], w["peer_vt"][l])
    return _final_norm(xf, w["g_final"]).reshape(B, S, D)


def kernel(x_prompt, x_sample, g_mix, g_ffn, g_final, w_in_ab, w_out_ab, lam_q1, lam_k1, lam_q2, lam_k2,
           g_subln, w_qkv_c, w_out_c, g_qnorm, g_knorm, w_peer_q, peer_sub_keys, peer_u, peer_v):
    w = dict(
        g_mix=g_mix, g_ffn=g_ffn, g_final=g_final,
        w_in_ab=w_in_ab.astype(bf16), w_out_ab=w_out_ab.astype(bf16),
        lam=jnp.stack([lam_q1, lam_k1, lam_q2, lam_k2], axis=1).astype(f32),
        g_subln=g_subln, w_qkv_c=w_qkv_c.astype(bf16), w_out_c=w_out_c.astype(bf16),
        g_qnorm=g_qnorm, g_knorm=g_knorm, w_peer_q=w_peer_q.astype(bf16),
        peer_sub_keys=peer_sub_keys, peer_u=peer_u.astype(bf16),
        peer_vt=jnp.swapaxes(peer_v.astype(bf16).reshape(peer_v.shape[0], -1, PEER_EB, peer_v.shape[2]), 2, 3),
    )
    return (_trunk(x_prompt, w), _trunk(x_sample, w))
```

```python
import functools
import math

import jax
import jax.numpy as jnp
from jax import lax
from jax.experimental import pallas as pl
from jax.experimental.pallas import tpu as pltpu

f32 = jnp.float32
bf16 = jnp.bfloat16

HEAD_DIM = 128
LANES = 128
SUBLANES = 8
MXU_N = 256
KEY_CHUNK = 512
ROPE_THETA = 10000.0
GRID_W = 64
EPS = 1e-6
NEG = -1e30
DIL_PAIRS = ((128, 1), (512, 4), (2048, 16))
CHUNK_UNROLL = 16
PEER_HEADS = 8
N_KEYS = 128
PEER_TOPK = 16
VMEM_LIMIT = 60 * 1024 * 1024

_NT = (((1,), (1,)), ((), ()))


def _largest_divisor(n, candidates):
    for c in candidates:
        if n % c == 0:
            return c
    return n


def _params(*sem, flags=None):
    return pltpu.CompilerParams(dimension_semantics=sem, vmem_limit_bytes=VMEM_LIMIT, flags=flags)


def _rms(x, g):
    return x * lax.rsqrt(jnp.mean(x * x, axis=-1, keepdims=True) + EPS) * g


def _norm_proj_kernel(x_ref, g_ref, w_ref, *rest, emit_h):
    if emit_h:
        o_ref, hb_ref, h_scr = rest
    else:
        o_ref, h_scr = rest

    @pl.when(pl.program_id(1) == 0)
    def _():
        hb = _rms(x_ref[...], g_ref[...]).astype(bf16)
        h_scr[...] = hb
        if emit_h:
            hb_ref[...] = hb

    o_ref[...] = jnp.dot(h_scr[...], w_ref[...], preferred_element_type=f32).astype(o_ref.dtype)


def _norm_proj(x, g, w, out_dtype, emit_h=False):
    T, D = x.shape
    N = w.shape[1]
    tm = _largest_divisor(T, (1024, 512, 256, 128))
    tn = _largest_divisor(N, (1024, 512, 256, 128))
    out_shape = [jax.ShapeDtypeStruct((T, N), out_dtype)]
    out_specs = [pl.BlockSpec((tm, tn), lambda i, j: (i, j))]
    if emit_h:
        out_shape.append(jax.ShapeDtypeStruct((T, D), bf16))
        out_specs.append(pl.BlockSpec((tm, D), lambda i, j: (i, 0)))
    res = pl.pallas_call(
        functools.partial(_norm_proj_kernel, emit_h=emit_h),
        grid=(T // tm, N // tn),
        in_specs=[pl.BlockSpec((tm, D), lambda i, j: (i, 0)),
                  pl.BlockSpec((1, D), lambda i, j: (0, 0)),
                  pl.BlockSpec((D, tn), lambda i, j: (0, j))],
        out_specs=out_specs,
        out_shape=out_shape,
        scratch_shapes=[pltpu.VMEM((tm, D), bf16)],
        compiler_params=_params("parallel", "arbitrary"),
        name="norm_proj",
    )(x, g.reshape(1, D), w)
    return res if emit_h else res[0]


def _out_proj_kernel(*refs, n_in):
    res_ref = refs[0]
    a_refs = refs[1:1 + n_in]
    w_refs = refs[1 + n_in:1 + 2 * n_in]
    o_ref = refs[1 + 2 * n_in]
    acc = res_ref[...]
    for a_ref, w_ref in zip(a_refs, w_refs):
        acc = acc + jnp.dot(a_ref[...], w_ref[...], preferred_element_type=f32)
    o_ref[...] = acc


def _out_proj(res, acts, w):
    T, N = res.shape
    tm = _largest_divisor(T, (1024, 512, 256, 128))
    tn = _largest_divisor(N, (1024, 512, 256, 128))
    n_in = len(acts)
    K = acts[0].shape[1]
    assert all(a.shape[1] == K for a in acts) and w.shape[0] == n_in * K
    in_specs = [pl.BlockSpec((tm, tn), lambda i, j: (i, j))]
    in_specs += [pl.BlockSpec((tm, K), lambda i, j: (i, 0)) for _ in acts]
    in_specs += [pl.BlockSpec((K, tn), functools.partial(lambda i, j, kk: (kk, j), kk=kk))
                 for kk in range(n_in)]
    return pl.pallas_call(
        functools.partial(_out_proj_kernel, n_in=n_in),
        grid=(T // tm, N // tn),
        in_specs=in_specs,
        out_specs=pl.BlockSpec((tm, tn), lambda i, j: (i, j)),
        out_shape=jax.ShapeDtypeStruct((T, N), f32),
        compiler_params=_params("parallel", "parallel"),
        name="out_proj",
    )(res, *acts, *([w] * n_in))


def _rope_tables(pos, dim):
    inv = ROPE_THETA ** (-jnp.arange(0, dim, 2, dtype=f32) / dim)
    ang = pos[:, None] * inv[None, :]
    ang = jnp.concatenate([ang, ang], axis=-1)
    return jnp.cos(ang), jnp.sin(ang)


def _rope_tables_full(S):
    cos, sin = _rope_tables(jnp.arange(S, dtype=f32), HEAD_DIM)
    half = HEAD_DIM // 2
    sign = jnp.where(jnp.arange(HEAD_DIM) < half, -1.0, 1.0).astype(f32)
    return cos, sin * sign[None, :]


def _rope_tables_axial(S):
    rows = S // GRID_W
    row_idx = jnp.repeat(jnp.arange(rows, dtype=f32), GRID_W)
    col_idx = jnp.tile(jnp.arange(GRID_W, dtype=f32), rows)
    cr, sr = _rope_tables(row_idx, HEAD_DIM // 2)
    cc, sc = _rope_tables(col_idx, HEAD_DIM // 2)
    quarter = HEAD_DIM // 4
    sign = jnp.where(jnp.arange(HEAD_DIM // 2) < quarter, -1.0, 1.0).astype(f32)
    cos = jnp.concatenate([cr, cc], axis=-1)
    sin = jnp.concatenate([sr * sign[None, :], sc * sign[None, :]], axis=-1)
    return cos, sin


def _rope(x, cos, sin_signed):
    return x * cos + pltpu.roll(x, HEAD_DIM // 2, axis=1) * sin_signed


def _rope_axial(x, cos, sin_signed):
    quarter = HEAD_DIM // 4
    lane = lax.broadcasted_iota(jnp.int32, x.shape, 1)
    lower = (lane % (2 * quarter)) < quarter
    rot = jnp.where(lower, pltpu.roll(x, HEAD_DIM - quarter, axis=1), pltpu.roll(x, quarter, axis=1))
    return x * cos + rot * sin_signed


def _softmax_pv(s, v):
    m = jnp.max(s, axis=-1, keepdims=True)
    p = jnp.exp(s - m)
    l = jnp.sum(p, axis=-1, keepdims=True)
    return jnp.dot(p.astype(bf16), v, preferred_element_type=f32) / l


def _attend_ones(q, k_ref, v_ones_ref):
    S, d2 = v_ones_ref.shape
    d = d2 // 2
    kc = min(S, KEY_CHUNK)
    m = acc = None
    for c0 in range(0, S, kc):
        s = lax.dot_general(q, k_ref[c0:c0 + kc, :], _NT, preferred_element_type=f32)
        m_c = jnp.max(s, axis=-1, keepdims=True)
        m_new = m_c if m is None else jnp.maximum(m, m_c)
        pv = jnp.dot(jnp.exp(s - m_new).astype(bf16), v_ones_ref[c0:c0 + kc, :], preferred_element_type=f32)
        acc = pv if acc is None else acc * jnp.exp(m - m_new) + pv
        m = m_new
    return acc[:, :d] / acc[:, d:]


def _attn_c_kernel(q_ref, k_ref, v_ref, cos_ref, sin_ref, cosq_ref, sinq_ref, gq_ref, gk_ref,
                   o_ref, k_scr, v_scr, *, groups, scale):
    @pl.when(pl.program_id(2) == 0)
    def _():
        k = _rms(k_ref[0].astype(f32), gk_ref[...])
        k_scr[...] = _rope_axial(k, cos_ref[...], sin_ref[...]).astype(bf16)
        v_scr[:, :HEAD_DIM] = v_ref[0]
        v_scr[:, HEAD_DIM:] = jnp.ones_like(v_ref[0])

    cq = cosq_ref[...]
    sq = sinq_ref[...]
    gq = gq_ref[...]
    for g in range(groups):
        cols = slice(g * HEAD_DIM, (g + 1) * HEAD_DIM)
        q = _rms(q_ref[0, :, cols].astype(f32), gq)
        q = (_rope_axial(q, cq, sq) * scale).astype(bf16)
        o_ref[0, :, cols] = _attend_ones(q, k_scr, v_scr).astype(o_ref.dtype)


def _attn_c(proj, g_q, g_k, cos, sin, n_heads, n_kv):
    B, S, _ = proj.shape
    groups = n_heads // n_kv
    tq = _largest_divisor(S, (256, 128))
    qw = groups * HEAD_DIM
    k_off = n_heads
    v_off = n_heads + n_kv
    return pl.pallas_call(
        functools.partial(_attn_c_kernel, groups=groups, scale=HEAD_DIM ** -0.5),
        grid=(B, n_kv, S // tq),
        in_specs=[pl.BlockSpec((1, tq, qw), lambda b, h, i: (b, i, h)),
                  pl.BlockSpec((1, S, HEAD_DIM), lambda b, h, i: (b, 0, k_off + h)),
                  pl.BlockSpec((1, S, HEAD_DIM), lambda b, h, i: (b, 0, v_off + h)),
                  pl.BlockSpec((S, HEAD_DIM), lambda b, h, i: (0, 0)),
                  pl.BlockSpec((S, HEAD_DIM), lambda b, h, i: (0, 0)),
                  pl.BlockSpec((tq, HEAD_DIM), lambda b, h, i: (i, 0)),
                  pl.BlockSpec((tq, HEAD_DIM), lambda b, h, i: (i, 0)),
                  pl.BlockSpec((1, HEAD_DIM), lambda b, h, i: (0, 0)),
                  pl.BlockSpec((1, HEAD_DIM), lambda b, h, i: (0, 0))],
        out_specs=pl.BlockSpec((1, tq, qw), lambda b, h, i: (b, i, h)),
        out_shape=jax.ShapeDtypeStruct((B, S, n_heads * HEAD_DIM), bf16),
        scratch_shapes=[pltpu.VMEM((S, HEAD_DIM), bf16), pltpu.VMEM((S, 2 * HEAD_DIM), bf16)],
        compiler_params=_params("parallel", "parallel", "arbitrary"),
        name="attn_c",
    )(proj, proj, proj, cos, sin, cos, sin, g_q.reshape(1, HEAD_DIM), g_k.reshape(1, HEAD_DIM))


def _attn_b_kernel(lam_ref, q_ref, k_ref, v_ref, cos_ref, sin_ref, cosq_ref, sinq_ref, gsub_ref,
                   o_ref, k_scr, *, scale, lam_init):
    @pl.when(pl.program_id(2) == 0)
    def _():
        for c in range(2):
            k = k_ref[0, :, c * HEAD_DIM:(c + 1) * HEAD_DIM].astype(f32)
            k_scr[c] = _rope(k, cos_ref[...], sin_ref[...]).astype(bf16)

    lam_p = lam_ref[...]
    lam = (jnp.exp(jnp.sum(lam_p[0:1] * lam_p[1:2], axis=-1, keepdims=True))
           - jnp.exp(jnp.sum(lam_p[2:3] * lam_p[3:4], axis=-1, keepdims=True)) + lam_init)
    cq = cosq_ref[...]
    sq = sinq_ref[...]
    v = v_ref[0]
    outs = []
    for c in range(2):
        q = q_ref[0, :, c * HEAD_DIM:(c + 1) * HEAD_DIM].astype(f32)
        q = (_rope(q, cq, sq) * scale).astype(bf16)
        s = lax.dot_general(q, k_scr[c], _NT, preferred_element_type=f32)
        outs.append(_softmax_pv(s, v))
    o = outs[0] - lam * outs[1]
    o = _rms(o, gsub_ref[...]) * (1.0 - lam_init)
    o_ref[0] = o.astype(o_ref.dtype)


def _attn_b(proj, lam_params, g_sub, cos, sin, lam_init, q_off, k_off, v_off, n_heads):
    B, S, _ = proj.shape
    tq = _largest_divisor(S, (256, 128))
    hw = 2 * HEAD_DIM
    return pl.pallas_call(
        functools.partial(_attn_b_kernel, scale=HEAD_DIM ** -0.5, lam_init=lam_init),
        grid=(B, n_heads, S // tq),
        in_specs=[pl.BlockSpec((4, HEAD_DIM), lambda b, h, i: (0, 0)),
                  pl.BlockSpec((1, tq, hw), lambda b, h, i: (b, i, q_off + h)),
                  pl.BlockSpec((1, S, hw), lambda b, h, i: (b, 0, k_off + h)),
                  pl.BlockSpec((1, S, hw), lambda b, h, i: (b, 0, v_off + h)),
                  pl.BlockSpec((S, HEAD_DIM), lambda b, h, i: (0, 0)),
                  pl.BlockSpec((S, HEAD_DIM), lambda b, h, i: (0, 0)),
                  pl.BlockSpec((tq, HEAD_DIM), lambda b, h, i: (i, 0)),
                  pl.BlockSpec((tq, HEAD_DIM), lambda b, h, i: (i, 0)),
                  pl.BlockSpec((1, hw), lambda b, h, i: (0, 0))],
        out_specs=pl.BlockSpec((1, tq, hw), lambda b, h, i: (b, i, h)),
        out_shape=jax.ShapeDtypeStruct((B, S, n_heads * hw), bf16),
        scratch_shapes=[pltpu.VMEM((2, S, HEAD_DIM), bf16)],
        compiler_params=_params("parallel", "parallel", "arbitrary"),
        name="attn_b",
    )(lam_params, proj, proj, proj, cos, sin, cos, sin, g_sub.reshape(1, hw))


def _attn_a_kernel(q_ref, k_ref, v_ref, cos_ref, sin_ref, o_ref, qf, kf, vf, *scr, S, scale):
    nb = len(DIL_PAIRS)
    o_scr, l_scr = scr[:nb], scr[nb:]
    cos = cos_ref[...]
    sin = sin_ref[...]
    qf[...] = _rope(q_ref[0].astype(f32), cos, sin) * scale
    kf[...] = _rope(k_ref[0].astype(f32), cos, sin)
    vf[...] = v_ref[0].astype(f32)

    for bi, (window, dil) in enumerate(DIL_PAIRS):
        L = S // dil
        half = window // (2 * dil)
        qc_n = min(128, L)
        kw_n = min(L, qc_n + 2 * half)
        n_chunks = L // qc_n

        def rows(start, n, dil=dil):
            return pl.ds(start, n) if dil == 1 else pl.ds(start, n, stride=dil)

        def body(idx, carry, dil=dil, half=half, qc_n=qc_n, kw_n=kw_n, n_chunks=n_chunks, L=L,
                 bi=bi, rows=rows):
            r = idx // n_chunks
            q0 = (idx % n_chunks) * qc_n
            k0 = jnp.clip(q0 - half, 0, L - kw_n)
            q_rows = rows(r + dil * q0, qc_n)
            k_rows = rows(r + dil * k0, kw_n)
            qc = qf[q_rows, :].astype(bf16)
            kc = kf[k_rows, :].astype(bf16)
            vc = vf[k_rows, :].astype(bf16)
            s = lax.dot_general(qc, kc, _NT, preferred_element_type=f32)
            qpos = q0 + lax.broadcasted_iota(jnp.int32, (qc_n, kw_n), 0)
            kpos = k0 + lax.broadcasted_iota(jnp.int32, (qc_n, kw_n), 1)
            s = jnp.where(jnp.abs(kpos - qpos) <= half, s, NEG)
            m = jnp.max(s, axis=-1, keepdims=True)
            p = jnp.exp(s - m)
            den = jnp.sum(p, axis=-1, keepdims=True)
            o_scr[bi][q_rows, :] = jnp.dot((p / den).astype(bf16), vc, preferred_element_type=f32)
            l_scr[bi][q_rows, :] = jnp.broadcast_to(m + jnp.log(den), (qc_n, HEAD_DIM))
            return carry

        n_steps = dil * n_chunks
        lax.fori_loop(0, n_steps, body, 0, unroll=math.gcd(n_steps, CHUNK_UNROLL))

    lses = [l_scr[bi][...] for bi in range(nb)]
    m = functools.reduce(jnp.maximum, lses)
    es = [jnp.exp(l - m) for l in lses]
    z = functools.reduce(lambda a, b: a + b, es)
    out = functools.reduce(lambda a, b: a + b, [(e / z) * o_scr[bi][...] for bi, e in enumerate(es)])
    o_ref[0] = out.astype(o_ref.dtype)


def _attn_a(proj, cos, sin, n_heads):
    B, S, _ = proj.shape
    nb = len(DIL_PAIRS)
    blk = lambda off: pl.BlockSpec((1, S, HEAD_DIM), lambda b, h: (b, 0, off + h))
    tab = pl.BlockSpec((S, HEAD_DIM), lambda b, h: (0, 0))
    return pl.pallas_call(
        functools.partial(_attn_a_kernel, S=S, scale=HEAD_DIM ** -0.5),
        grid=(B, n_heads),
        in_specs=[blk(0), blk(n_heads), blk(2 * n_heads), tab, tab],
        out_specs=pl.BlockSpec((1, S, HEAD_DIM), lambda b, h: (b, 0, h)),
        out_shape=jax.ShapeDtypeStruct((B, S, n_heads * HEAD_DIM), bf16),
        scratch_shapes=[pltpu.VMEM((S, HEAD_DIM), f32), pltpu.VMEM((S, HEAD_DIM), f32),
                        pltpu.VMEM((S, HEAD_DIM), f32)]
                       + [pltpu.VMEM((S, HEAD_DIM), f32) for _ in range(2 * nb)],
        compiler_params=_params("parallel", "parallel"),
        name="attn_a",
    )(proj, proj, proj, cos, sin)


def _take_topk(s, order, k, break_ties):
    last = jnp.iinfo(jnp.int32).max
    taken = jnp.full(s.shape, k, jnp.int32)
    vals = []
    for i in range(k):
        m = jnp.max(s, axis=0, keepdims=True)
        hit = s == m
        if break_ties:
            hit = order == jnp.min(jnp.where(hit, order, last), axis=0, keepdims=True)
        s = jnp.where(hit, -jnp.inf, s)
        taken = jnp.where(hit, i, taken)
        vals.append(m)
    return jnp.concatenate(vals, axis=0), taken


def _peer_tables(scores, break_ties):
    K = PEER_TOPK
    half = K // 2
    n = scores[0].shape[1]
    tops, ranks = [], []
    for s in scores:
        top, rank = _take_topk(s, lax.broadcasted_iota(jnp.int32, s.shape, 0), K, break_ties)
        tops.append(top)
        ranks.append(rank)
    v1, v2 = tops
    i16 = lax.broadcasted_iota(jnp.int32, (K, n), 0)
    i8 = lax.broadcasted_iota(jnp.int32, (half, n), 0)
    cand = [v1 + v2[0:1]]
    order = [i16 * K]
    for j in range(1, half):
        cand.append(v1[0:half] + v2[j:j + 1])
        order.append(i8 * K + j)
    cand.append(v1[0:1] + v2[half:K])
    order.append(i8 + half)
    top, taken = _take_topk(jnp.concatenate(cand, axis=0), jnp.concatenate(order, axis=0), K, break_ties)
    kept = (taken < K).astype(f32)
    upper = kept[0:half]
    for j in range(1, half):
        upper = upper + kept[K + (j - 1) * half:K + j * half]
    row0 = jnp.sum(kept[K + (half - 1) * half:], axis=0, keepdims=True)
    upper = upper + jnp.where(i8 == 0, row0, 0.0)
    width = jnp.concatenate([upper, kept[half:K]], axis=0)
    width_of_key = jnp.zeros(scores[0].shape, f32)
    for i in range(K):
        width_of_key = jnp.where(ranks[0] == i, width[i:i + 1], width_of_key)
    z = jnp.sum(jnp.exp(top - top[0:1]), axis=0, keepdims=True)
    tables = (width_of_key, jnp.exp(scores[0] - v1[0:1]) / z,
              ranks[1].astype(f32), jnp.exp(scores[1] - v2[0:1]))
    n_taken = [jnp.sum((r < K).astype(f32), axis=0, keepdims=True) for r in ranks]
    n_taken.append(jnp.sum(kept, axis=0, keepdims=True))
    over = jnp.max(functools.reduce(jnp.maximum, n_taken)) > K
    return tables, over


def _peer_retrieve_kernel(q_ref, keys_ref, p1_ref, p2_ref):
    scores = []
    for c in range(2):
        qc = q_ref[:, c * N_KEYS:(c + 1) * N_KEYS].astype(bf16)
        kc = keys_ref[0, c].astype(bf16)
        scores.append(lax.dot_general(kc, qc, _NT, preferred_element_type=f32))

    def write(tables):
        p1_ref[0, 0], p1_ref[0, 1], p2_ref[0, 0], p2_ref[0, 1] = tables

    tables, tied = _peer_tables(scores, break_ties=False)
    write(tables)

    @pl.when(tied)
    def _():
        write(_peer_tables(scores, break_ties=True)[0])


def _peer_retrieve(q, sub_keys):
    T = q.shape[0]
    tm = _largest_divisor(T, (256, 128))
    p_spec = pl.BlockSpec((1, 2, N_KEYS, tm), lambda i, h: (h, 0, 0, i))
    p_shape = jax.ShapeDtypeStruct((PEER_HEADS, 2, N_KEYS, T), f32)
    return pl.pallas_call(
        _peer_retrieve_kernel,
        grid=(T // tm, PEER_HEADS),
        in_specs=[pl.BlockSpec((tm, 2 * N_KEYS), lambda i, h: (i, h)),
                  pl.BlockSpec((1, 2, N_KEYS, HEAD_DIM), lambda i, h: (h, 0, 0, 0))],
        out_specs=[p_spec, p_spec],
        out_shape=[p_shape, p_shape],
        compiler_params=_params("parallel", "parallel"),
        name="peer_retrieve",
    )(q, sub_keys)


PEER_EB = 1024
GATE_ROWS = 32


def _gelu(x):
    return 0.5 * x * (1.0 + lax.erf(x * (1.0 / math.sqrt(2.0))))


def _peer_expert_kernel(x_ref, hb_ref, p1_ref, p2_ref, u_ref, vt_ref, o_ref, acc_ref, at_ref):
    j = pl.program_id(1)
    eb, tm = at_ref.shape
    heads = range(PEER_HEADS)
    group = 2 * N_KEYS

    @pl.when(j == 0)
    def _():
        acc_ref[...] = jnp.zeros_like(acc_ref)

    def body(g, carry):
        u_rows = u_ref[pl.ds(pl.multiple_of(g * group, group), group), :]
        w1_rows = [[p1_ref[h, 0, pl.ds(g * (group // N_KEYS) + aa, 1), :] for h in heads]
                   for aa in range(group // N_KEYS)]
        e1_rows = [[p1_ref[h, 1, pl.ds(g * (group // N_KEYS) + aa, 1), :] for h in heads]
                   for aa in range(group // N_KEYS)]
        for t0 in range(0, tm, MXU_N):
            ht = lax.dot_general(u_rows, hb_ref[t0:t0 + MXU_N, :], _NT, preferred_element_type=f32)
            for aa in range(group // N_KEYS):
                sb = g * (group // N_KEYS) + aa
                for l0 in range(t0, t0 + MXU_N, LANES):
                    lanes = slice(l0, l0 + LANES)
                    w1 = [row[:, lanes] for row in w1_rows[aa]]
                    e1 = [row[:, lanes] for row in e1_rows[aa]]
                    for r in range(0, N_KEYS, GATE_ROWS):
                        krows = slice(r, r + GATE_ROWS)
                        gate = None
                        for h in heads:
                            hit = p2_ref[h, 0, krows, lanes] < w1[h]
                            w = jnp.where(hit, e1[h] * p2_ref[h, 1, krows, lanes], 0.0)
                            gate = w if gate is None else gate + w
                        hrows = slice(aa * N_KEYS + r, aa * N_KEYS + r + GATE_ROWS)
                        erows = pl.ds(pl.multiple_of(sb * N_KEYS + r, GATE_ROWS), GATE_ROWS)
                        act = gate * _gelu(ht[hrows, l0 - t0:l0 - t0 + LANES])
                        at_ref[erows, lanes] = act.astype(bf16)
        return carry

    lax.fori_loop(0, eb // group, body, 0)
    acc_ref[...] += jnp.dot(vt_ref[...], at_ref[...], preferred_element_type=f32)

    @pl.when(j == pl.num_programs(1) - 1)
    def _():
        o_ref[...] = x_ref[...] + acc_ref[...].T


def _peer_expert(x, hb, p1, p2, u, vt):
    T, D = x.shape
    E = u.shape[0]
    tm = _largest_divisor(T, (512, 256, 128))
    eb = PEER_EB
    return pl.pallas_call(
        _peer_expert_kernel,
        grid=(T // tm, E // eb),
        in_specs=[pl.BlockSpec((tm, D), lambda i, j: (i, 0)),
                  pl.BlockSpec((tm, D), lambda i, j: (i, 0)),
                  pl.BlockSpec((PEER_HEADS, 2, eb // N_KEYS, tm), lambda i, j: (0, 0, j, i)),
                  pl.BlockSpec((PEER_HEADS, 2, N_KEYS, tm), lambda i, j: (0, 0, 0, i)),
                  pl.BlockSpec((eb, D), lambda i, j: (j, 0)),
                  pl.BlockSpec((None, D, eb), lambda i, j: (j, 0, 0))],
        out_specs=pl.BlockSpec((tm, D), lambda i, j: (i, 0)),
        out_shape=jax.ShapeDtypeStruct((T, D), f32),
        scratch_shapes=[pltpu.VMEM((D, tm), f32), pltpu.VMEM((eb, tm), bf16)],
        compiler_params=_params("parallel", "arbitrary"),
        name="peer_expert",
    )(x, hb, p1, p2, u, vt)


def _final_norm_kernel(x_ref, g_ref, o_ref):
    o_ref[...] = _rms(x_ref[...], g_ref[...])


def _final_norm(x, g):
    T, D = x.shape
    tm = _largest_divisor(T, (512, 256, 128))
    return pl.pallas_call(
        _final_norm_kernel,
        grid=(T // tm,),
        in_specs=[pl.BlockSpec((tm, D), lambda i: (i, 0)), pl.BlockSpec((1, D), lambda i: (0, 0))],
        out_specs=pl.BlockSpec((tm, D), lambda i: (i, 0)),
        out_shape=jax.ShapeDtypeStruct((T, D), f32),
        compiler_params=_params("parallel"),
        name="final_norm",
    )(x, g.reshape(1, D))


def _trunk(x, w):
    B, S, D = x.shape
    T = B * S
    depth = w["g_mix"].shape[0]
    heads_a = D // (2 * HEAD_DIM)
    heads_b = D // (4 * HEAD_DIM)
    heads_c = D // HEAD_DIM
    kv_c = heads_c // 4
    cos, sin = _rope_tables_full(S)
    cos_ax, sin_ax = _rope_tables_axial(S)
    xf = x.reshape(T, D)
    for l in range(depth):
        if l % 2 == 0:
            e = l // 2
            lam_init = 0.8 - 0.6 * math.exp(-0.3 * l)
            proj = _norm_proj(xf, w["g_mix"][l], w["w_in_ab"][e], bf16).reshape(B, S, -1)
            oa = _attn_a(proj, cos, sin, heads_a)
            q_off = 3 * heads_a // 2
            ob = _attn_b(proj, w["lam"][e], w["g_subln"][e], cos, sin, lam_init,
                         q_off, q_off + heads_b, q_off + 2 * heads_b, heads_b)
            xf = _out_proj(xf, [oa.reshape(T, -1), ob.reshape(T, -1)], w["w_out_ab"][e])
        else:
            o = l // 2
            proj = _norm_proj(xf, w["g_mix"][l], w["w_qkv_c"][o], bf16).reshape(B, S, -1)
            oc = _attn_c(proj, w["g_qnorm"][o], w["g_knorm"][o], cos_ax, sin_ax, heads_c, kv_c)
            xf = _out_proj(xf, [oc.reshape(T, -1)], w["w_out_c"][o])
        q, hb = _norm_proj(xf, w["g_ffn"][l], w["w_peer_q"][l], f32, emit_h=True)
        p1, p2 = _peer_retrieve(q, w["peer_sub_keys"][l])
        xf = _peer_expert(xf, hb, p1, p2, w["peer_u"][l], w["peer_vt"][l])
    return _final_norm(xf, w["g_final"]).reshape(B, S, D)


def kernel(x_prompt, x_sample, g_mix, g_ffn, g_final, w_in_ab, w_out_ab, lam_q1, lam_k1, lam_q2, lam_k2,
           g_subln, w_qkv_c, w_out_c, g_qnorm, g_knorm, w_peer_q, peer_sub_keys, peer_u, peer_v):
    w = dict(
        g_mix=g_mix, g_ffn=g_ffn, g_final=g_final,
        w_in_ab=w_in_ab.astype(bf16), w_out_ab=w_out_ab.astype(bf16),
        lam=jnp.stack([lam_q1, lam_k1, lam_q2, lam_k2], axis=1).astype(f32),
        g_subln=g_subln, w_qkv_c=w_qkv_c.astype(bf16), w_out_c=w_out_c.astype(bf16),
        g_qnorm=g_qnorm, g_knorm=g_knorm, w_peer_q=w_peer_q.astype(bf16),
        peer_sub_keys=peer_sub_keys, peer_u=peer_u.astype(bf16),
        peer_vt=jnp.swapaxes(peer_v.astype(bf16).reshape(peer_v.shape[0], -1, PEER_EB, peer_v.shape[2]), 2, 3),
    )
    return (_trunk(x_prompt, w), _trunk(x_sample, w))
```
